```python
import jax, jax.numpy as jnp
from jax import lax
import numpy as np

D_MODEL = 1024
BATCH = 16
SEQ = 2048
DEPTH = 1
DEC_BATCH = 8
DEC_SEQ = 2048
PAST_LEN = 128

HEAD_DIM = 64
N_HEADS_NA = 8
N_HEADS_DIL = 8
MIX_NA = N_HEADS_NA * HEAD_DIM
MIX_DIL = N_HEADS_DIL * HEAD_DIM
MIX_WIDTH = MIX_NA + MIX_DIL
GRID_W = 64
NA_ROWS = 8
NA_COLS = 16
DIL_PATTERNS = ((128, 1), (512, 4), (2048, 16))
DIL_BLOCK = 64
N_EXPERTS = 32
TOP_K = 4
D_FF = D_MODEL
SWIGLU_LIMIT = 7.0
SWIGLU_ALPHA = 1.702
MOE_BLOCK = 128
RMS_EPS = 1e-5

kernel_name = "hybrid_na2d_dilated_moe_encoder"


def rmsnorm(x, g):
    xf = x.astype(jnp.float32)
    y = xf * lax.rsqrt(jnp.mean(xf * xf, axis=-1, keepdims=True) + RMS_EPS)
    return (y * g.astype(jnp.float32)).astype(x.dtype)


def alibi_slopes(n_heads):
    return jnp.exp2(-8.0 * jnp.arange(1, n_heads + 1, dtype=jnp.float32) / n_heads)


def neighbourhood_attention_2d(q, k, v, rpb):
    B, T, H, hd = q.shape
    R = T // GRID_W
    kh = min(NA_ROWS, R)
    kw = NA_COLS
    scale = HEAD_DIM ** -0.5
    qg = q.reshape(B, R, GRID_W, H, hd)
    rows = jnp.arange(R)
    row_start = jnp.clip(rows - kh // 2, 0, R - kh)
    row_idx = row_start[:, None] + jnp.arange(kh)[None, :]
    kg = k.reshape(B, R, GRID_W, H, hd)[:, row_idx]
    vg = v.reshape(B, R, GRID_W, H, hd)[:, row_idx]
    s = jnp.einsum('brchd,brawhd->bhrcaw', qg, kg).astype(jnp.float32) * scale
    cols = jnp.arange(GRID_W)
    col_start = jnp.clip(cols - kw // 2, 0, GRID_W - kw)
    col_ok = (cols[None, :] >= col_start[:, None]) & (cols[None, :] < col_start[:, None] + kw)
    dr_idx = row_idx - rows[:, None] + (NA_ROWS - 1)
    dc_idx = jnp.clip(cols[None, :] - cols[:, None] + (kw - 1), 0, 2 * kw - 2)
    bias = rpb.astype(jnp.float32)[:, dr_idx[:, None, :, None], dc_idx[None, :, None, :]]
    s = jnp.where(col_ok[None, None, None, :, None, :], s + bias[None], -jnp.inf)
    p = jax.nn.softmax(s, axis=(-2, -1))
    o = jnp.einsum('bhrcaw,brawhd->brchd', p, vg.astype(jnp.float32))
    return o.reshape(B, T, H, hd).astype(q.dtype)


def dilated_branch(q, k, v, slopes, window, dil):
    B, T, H, hd = q.shape
    half = window // (2 * dil)
    L = T // dil
    nb = -(-L // DIL_BLOCK)
    Lp = nb * DIL_BLOCK
    scale = HEAD_DIM ** -0.5
    qs = q.reshape(B, L, dil, H, hd)
    ks = k.reshape(B, L, dil, H, hd)
    vs = v.reshape(B, L, dil, H, hd)
    qb = jnp.pad(qs, ((0, 0), (0, Lp - L), (0, 0), (0, 0), (0, 0))).reshape(B, nb, DIL_BLOCK, dil, H, hd)
    pad_k = ((0, 0), (DIL_BLOCK, Lp - L + DIL_BLOCK), (0, 0), (0, 0), (0, 0))
    kp = jnp.pad(ks, pad_k)
    vp = jnp.pad(vs, pad_k)
    kidx = jnp.arange(nb)[:, None] * DIL_BLOCK + jnp.arange(3 * DIL_BLOCK)[None, :]
    kb = kp[:, kidx]
    vb = vp[:, kidx]
    s = jnp.einsum('bnqrhd,bnkrhd->bhrnqk', qb, kb).astype(jnp.float32) * scale
    a = jnp.arange(DIL_BLOCK)
    c = jnp.arange(3 * DIL_BLOCK)
    delta = c[None, :] - DIL_BLOCK - a[:, None]
    j = jnp.arange(nb)[:, None] * DIL_BLOCK - DIL_BLOCK + c[None, :]
    ok = (jnp.abs(delta)[None] <= half) & ((j >= 0) & (j < L))[:, None, :]
    dist = (dil * jnp.abs(delta)).astype(jnp.float32)
    alibi = -slopes[:, None, None, None, None] * dist[None, None, None]
    s = jnp.where(ok[None, None, None], s + alibi[None], -jnp.inf)
    lse = jax.nn.logsumexp(s, axis=-1)
    p = jnp.exp(s - lse[..., None])
    o = jnp.einsum('bhrnqk,bnkrhd->bnqrhd', p, vb.astype(jnp.float32))
    o = o.reshape(B, Lp, dil, H, hd)[:, :L].reshape(B, T, H, hd)
    lse = jnp.transpose(lse, (0, 3, 4, 2, 1)).reshape(B, Lp, dil, H)[:, :L].reshape(B, T, H)
    return o, lse


def dilated_attention(q, k, v):
    slopes = alibi_slopes(q.shape[2])
    outs, lses = [], []
    for window, dil in DIL_PATTERNS:
        o, l = dilated_branch(q, k, v, slopes, window, dil)
        outs.append(o)
        lses.append(l)
    w = jax.nn.softmax(jnp.stack(lses, 0), axis=0)
    o = jnp.sum(w[..., None] * jnp.stack(outs, 0), axis=0)
    return o.astype(q.dtype)


def moe_ffn(x2, w_router, b_router, w_gate, b_gate, w_up, b_up, w_down, b_down):
    N, D = x2.shape
    logits = (x2 @ w_router + b_router).astype(jnp.float32)
    top_vals, top_idx = lax.top_k(logits, TOP_K)
    gates = jax.nn.softmax(top_vals, axis=-1)
    flat_e = top_idx.reshape(-1)
    flat_tok = jnp.arange(N * TOP_K, dtype=jnp.int32) // TOP_K
    flat_g = gates.reshape(-1)
    order = jnp.argsort(flat_e, stable=True)
    sorted_e = flat_e[order]
    counts = jnp.bincount(flat_e, length=N_EXPERTS)
    starts = jnp.cumsum(counts) - counts
    padded_counts = (counts + MOE_BLOCK - 1) // MOE_BLOCK * MOE_BLOCK
    padded_ends = jnp.cumsum(padded_counts)
    padded_starts = padded_ends - padded_counts
    rank = jnp.arange(N * TOP_K) - starts[sorted_e]
    dest = padded_starts[sorted_e] + rank
    P = N * TOP_K + N_EXPERTS * MOE_BLOCK
    n_blocks = P // MOE_BLOCK
    tok_pad = jnp.full((P,), N, dtype=jnp.int32).at[dest].set(flat_tok[order])
    gate_pad = jnp.zeros((P,), jnp.float32).at[dest].set(flat_g[order])
    block_expert = jnp.clip(jnp.searchsorted(padded_ends, jnp.arange(n_blocks) * MOE_BLOCK, side='right'), 0, N_EXPERTS - 1)
    x_pad = jnp.concatenate([x2, jnp.zeros((1, D), x2.dtype)], axis=0)
    xb = x_pad[tok_pad].reshape(n_blocks, MOE_BLOCK, D)

    def expert_block(args):
        xblk, e = args
        g = xblk @ w_gate[e] + b_gate[e]
        u = xblk @ w_up[e] + b_up[e]
        g = jnp.minimum(g, SWIGLU_LIMIT)
        u = jnp.clip(u, -SWIGLU_LIMIT, SWIGLU_LIMIT)
        h = (u + 1.0) * (g * jax.nn.sigmoid(SWIGLU_ALPHA * g))
        return h @ w_down[e] + b_down[e]

    yb = lax.map(expert_block, (xb, block_expert)).reshape(P, D)
    y = jnp.zeros((N + 1, D), x2.dtype).at[tok_pad].add(yb * gate_pad[:, None].astype(yb.dtype))
    return y[:N]


def encoder_trunk(x, norm1_g, w_in, rpb, gnorm_na, gnorm_dil, w_out, norm2_g,
                  w_router, b_router, w_gate, b_gate, w_up, b_up, w_down, b_down, final_g):
    B, T, D = x.shape
    for l in range(DEPTH):
        xn = rmsnorm(x, norm1_g[l])
        qkv = xn @ w_in[l]
        q, k, v = jnp.split(qkv, 3, axis=-1)
        heads_na = lambda a: a[..., :MIX_NA].reshape(B, T, N_HEADS_NA, HEAD_DIM)
        heads_dil = lambda a: a[..., MIX_NA:].reshape(B, T, N_HEADS_DIL, HEAD_DIM)
        o_na = neighbourhood_attention_2d(heads_na(q), heads_na(k), heads_na(v), rpb[l]).reshape(B, T, MIX_NA)
        o_dil = dilated_attention(heads_dil(q), heads_dil(k), heads_dil(v)).reshape(B, T, MIX_DIL)
        mixed = jnp.concatenate([rmsnorm(o_na, gnorm_na[l]), rmsnorm(o_dil, gnorm_dil[l])], axis=-1)
        x = x + mixed @ w_out[l]
        hn = rmsnorm(x, norm2_g[l]).reshape(B * T, D)
        x = x + moe_ffn(hn, w_router[l], b_router[l], w_gate[l], b_gate[l], w_up[l], b_up[l],
                        w_down[l], b_down[l]).reshape(B, T, D)
    return rmsnorm(x, final_g)


def setup_inputs(seed: int = 0) -> dict:
    key = jax.random.key(seed)
    ks = jax.random.split(key, 20)
    f32 = jnp.float32
    nrm = lambda k, shape, s: jax.random.normal(k, shape, f32) * s
    return {
        "x_prompt": nrm(ks[0], (BATCH, SEQ, D_MODEL), 1.0),
        "x_sample": nrm(ks[1], (DEC_BATCH, DEC_SEQ, D_MODEL), 1.0),
        "norm1_g": 1.0 + nrm(ks[2], (DEPTH, D_MODEL), 0.1),
        "w_in": nrm(ks[3], (DEPTH, D_MODEL, 3 * MIX_WIDTH), D_MODEL ** -0.5),
        "rpb": nrm(ks[4], (DEPTH, N_HEADS_NA, 2 * NA_ROWS - 1, 2 * NA_COLS - 1), 0.1),
        "gnorm_na": 1.0 + nrm(ks[5], (DEPTH, MIX_NA), 0.1),
        "gnorm_dil": 1.0 + nrm(ks[6], (DEPTH, MIX_DIL), 0.1),
        "w_out": nrm(ks[7], (DEPTH, MIX_WIDTH, D_MODEL), MIX_WIDTH ** -0.5),
        "norm2_g": 1.0 + nrm(ks[8], (DEPTH, D_MODEL), 0.1),
        "w_router": nrm(ks[9], (DEPTH, D_MODEL, N_EXPERTS), D_MODEL ** -0.5),
        "b_router": nrm(ks[10], (DEPTH, N_EXPERTS), 0.01),
        "w_gate": nrm(ks[11], (DEPTH, N_EXPERTS, D_MODEL, D_FF), D_MODEL ** -0.5),
        "b_gate": nrm(ks[12], (DEPTH, N_EXPERTS, D_FF), 0.02),
        "w_up": nrm(ks[13], (DEPTH, N_EXPERTS, D_MODEL, D_FF), D_MODEL ** -0.5),
        "b_up": nrm(ks[14], (DEPTH, N_EXPERTS, D_FF), 0.02),
        "w_down": nrm(ks[15], (DEPTH, N_EXPERTS, D_FF, D_MODEL), D_FF ** -0.5),
        "b_down": nrm(ks[16], (DEPTH, N_EXPERTS, D_MODEL), 0.02),
        "final_g": 1.0 + nrm(ks[17], (D_MODEL,), 0.1),
    }


def reference(x_prompt, x_sample, norm1_g, w_in, rpb, gnorm_na, gnorm_dil, w_out, norm2_g,
              w_router, b_router, w_gate, b_gate, w_up, b_up, w_down, b_down, final_g):
    y_prompt = encoder_trunk(x_prompt, norm1_g, w_in, rpb, gnorm_na, gnorm_dil, w_out, norm2_g,
                             w_router, b_router, w_gate, b_gate, w_up, b_up, w_down, b_down, final_g)
    y_sample = encoder_trunk(x_sample, norm1_g, w_in, rpb, gnorm_na, gnorm_dil, w_out, norm2_g,
                             w_router, b_router, w_gate, b_gate, w_up, b_up, w_down, b_down, final_g)
    return (y_prompt, y_sample)
```

```python
import functools

import jax
import jax.numpy as jnp
from jax import lax
from jax.experimental import pallas as pl
from jax.experimental.pallas import tpu as pltpu

F32 = jnp.float32
BF16 = jnp.bfloat16

HEAD_DIM = 64
N_HEADS_NA = 8
N_HEADS_DIL = 8
MIX_NA = N_HEADS_NA * HEAD_DIM
MIX_DIL = N_HEADS_DIL * HEAD_DIM
MIX_WIDTH = MIX_NA + MIX_DIL
GRID_W = 64
NA_ROWS = 8
NA_COLS = 16
DIL_PATTERNS = ((128, 1), (512, 4), (2048, 16))
N_EXPERTS = 32
TOP_K = 4
SWIGLU_LIMIT = 7.0
SWIGLU_ALPHA = 1.702
RMS_EPS = 1e-5

LANES = 128
NEG_BIG = -1e30
VMEM_LIMIT = 56 * 1024 * 1024
TOKEN_TILE = 512
EXPERT_BLOCK = 256


def _cparams(*sem):
    return pltpu.CompilerParams(dimension_semantics=sem, vmem_limit_bytes=VMEM_LIMIT)


def _qkv_kernel(x_ref, g_ref, w_ref, o_ref):
    x = x_ref[...]
    ms = jnp.mean(x * x, axis=-1, keepdims=True)
    xn = (x * lax.rsqrt(ms + RMS_EPS) * g_ref[...]).astype(BF16)
    n_out = o_ref.shape[1]
    chunk = 512
    for j in range(n_out // chunk):
        acc = jnp.dot(xn, w_ref[:, j * chunk:(j + 1) * chunk], preferred_element_type=F32)
        if (j + 1) * chunk <= MIX_WIDTH:
            acc = acc * (HEAD_DIM ** -0.5)
        o_ref[:, j * chunk:(j + 1) * chunk] = acc.astype(BF16)


def _qkv_proj(x2, g1, w_in):
    n, d = x2.shape
    n_out = w_in.shape[1]
    return pl.pallas_call(
        _qkv_kernel,
        out_shape=jax.ShapeDtypeStruct((n, n_out), BF16),
        grid=(n // TOKEN_TILE,),
        in_specs=[
            pl.BlockSpec((TOKEN_TILE, d), lambda i: (i, 0)),
            pl.BlockSpec((1, d), lambda i: (0, 0)),
            pl.BlockSpec((d, n_out), lambda i: (0, 0)),
        ],
        out_specs=pl.BlockSpec((TOKEN_TILE, n_out), lambda i: (i, 0)),
        compiler_params=_cparams("arbitrary"),
        name="qkv_proj",
    )(x2, g1.reshape(1, d), w_in.astype(BF16))


def _na_bias_table(rpb):
    kw = NA_COLS
    cols = jnp.arange(GRID_W)
    col_start = jnp.clip(cols - kw // 2, 0, GRID_W - kw)
    col_ok = (cols[None, :] >= col_start[:, None]) & (cols[None, :] < col_start[:, None] + kw)
    dc_idx = jnp.clip(cols[None, :] - cols[:, None] + (kw - 1), 0, 2 * kw - 2)
    b = rpb.astype(F32)[:, :, dc_idx]
    b = jnp.where(col_ok[None, None], b, NEG_BIG)
    slabs = [b[:, o:o + NA_ROWS] for o in range(NA_ROWS)]
    t = jnp.stack(slabs, 0)
    t = jnp.transpose(t, (0, 1, 3, 2, 4))
    return t.reshape(NA_ROWS, rpb.shape[0], GRID_W, NA_ROWS * GRID_W)


def _softmax_pv(s, v):
    m = jnp.max(s, axis=-1, keepdims=True)
    p = jnp.exp(s - m)
    l = jnp.sum(p, axis=-1, keepdims=True)
    o = jnp.dot(p.astype(BF16), v, preferred_element_type=F32)
    return o, m, l


def _na_kernel(q_ref, k_ref, v_ref, b_ref, o_ref):
    t = q_ref.shape[1]
    n_rows = t // GRID_W
    kh = NA_ROWS
    lane = lax.broadcasted_iota(jnp.int32, (GRID_W, LANES), 1)
    head0 = lane < HEAD_DIM
    for p in range(MIX_NA // LANES):
        cs = slice(p * LANES, (p + 1) * LANES)

        def row_body(r, carry, p=p, cs=cs):
            rs = jnp.clip(r - kh // 2, 0, n_rows - kh)
            off = rs - r + (NA_ROWS - 1)
            q0 = pl.multiple_of(r * GRID_W, GRID_W)
            k0 = pl.multiple_of(rs * GRID_W, GRID_W)
            q = q_ref[0, pl.ds(q0, GRID_W), cs]
            kwin = k_ref[0, pl.ds(k0, kh * GRID_W), cs]
            vwin = v_ref[0, pl.ds(k0, kh * GRID_W), cs]
            outs = []
            for h in range(2):
                qh = jnp.where(head0 if h == 0 else ~head0, q, jnp.zeros_like(q))
                s = lax.dot_general(qh, kwin, (((1,), (1,)), ((), ())), preferred_element_type=F32)
                s = s + b_ref[off, 2 * p + h]
                o, _, l = _softmax_pv(s, vwin)
                outs.append(o / l)
            o_ref[0, pl.ds(q0, GRID_W), cs] = jnp.where(head0, outs[0], outs[1]).astype(o_ref.dtype)
            return carry

        lax.fori_loop(0, n_rows, row_body, 0)


def _na_attention(qkv3, bias_tab):
    b, t, _ = qkv3.shape
    qblk = MIX_WIDTH // MIX_NA
    spec = lambda c: pl.BlockSpec((1, t, MIX_NA), lambda i, c=c: (i, 0, c))
    return pl.pallas_call(
        _na_kernel,
        out_shape=jax.ShapeDtypeStruct((b, t, MIX_NA), BF16),
        grid=(b,),
        in_specs=[spec(0), spec(qblk), spec(2 * qblk),
                  pl.BlockSpec(bias_tab.shape, lambda i: (0, 0, 0, 0))],
        out_specs=pl.BlockSpec((1, t, MIX_NA), lambda i: (i, 0, 0)),
        compiler_params=_cparams("arbitrary"),
        name="na_attention",
    )(qkv3, qkv3, qkv3, bias_tab)


def _band_kernel(q_ref, k_ref, v_ref, o_ref, lse_ref, *, dil, half, qb, kwin):
    gs, s_len, _ = q_ref.shape
    n_qb = s_len // qb
    row = lax.broadcasted_iota(jnp.int32, (qb, kwin), 0)
    col = lax.broadcasted_iota(jnp.int32, (qb, kwin), 1)
    lane = lax.broadcasted_iota(jnp.int32, (qb, LANES), 1)
    head0 = lane < HEAD_DIM
    for p in range(MIX_DIL // LANES):
        cs = slice(p * LANES, (p + 1) * LANES)

        def body(it, carry, p=p, cs=cs):
            g = it // n_qb
            qs = (it % n_qb) * qb
            ks = jnp.clip(qs - half, 0, s_len - kwin)
            qs = pl.multiple_of(qs, qb)
            ks = pl.multiple_of(ks, half)
            q = q_ref[g, pl.ds(qs, qb), cs]
            kw = k_ref[g, pl.ds(ks, kwin), cs]
            vw = v_ref[g, pl.ds(ks, kwin), cs]
            dist = jnp.abs(col - row + (ks - qs))
            ok = dist <= half
            fdist = dist.astype(F32)
            outs, lses = [], []
            for h in range(2):
                slope = 2.0 ** (-8.0 * (2 * p + h + 1) / N_HEADS_DIL) * dil
                qh = jnp.where(head0 if h == 0 else ~head0, q, jnp.zeros_like(q))
                s = lax.dot_general(qh, kw, (((1,), (1,)), ((), ())), preferred_element_type=F32)
                s = jnp.where(ok, s - slope * fdist, NEG_BIG)
                o, m, l = _softmax_pv(s, vw)
                outs.append(o / l)
                lses.append(m + jnp.log(l))
            o_ref[g, pl.ds(qs, qb), cs] = jnp.where(head0, outs[0], outs[1]).astype(o_ref.dtype)
            lse_ref[g, pl.ds(qs, qb), cs] = jnp.where(head0, lses[0], lses[1])
            return carry

        lax.fori_loop(0, gs * n_qb, body, 0)


def _band_attention(q_arr, k_arr, v_arr, cols, s_len, gs, dil, half):
    g = q_arr.shape[0]
    qb = min(128, s_len)
    kwin = min(qb + 2 * half, s_len)
    spec = lambda c: pl.BlockSpec((gs, s_len, MIX_DIL), lambda i, c=c: (i, 0, c))
    ospec = pl.BlockSpec((gs, s_len, MIX_DIL), lambda i: (i, 0, 0))
    return pl.pallas_call(
        functools.partial(_band_kernel, dil=dil, half=half, qb=qb, kwin=kwin),
        out_shape=(jax.ShapeDtypeStruct((g, s_len, MIX_DIL), BF16),
                   jax.ShapeDtypeStruct((g, s_len, MIX_DIL), F32)),
        grid=(g // gs,),
        in_specs=[spec(cols[0]), spec(cols[1]), spec(cols[2])],
        out_specs=(ospec, ospec),
        compiler_params=_cparams("arbitrary"),
        name=f"band_attention_d{dil}",
    )(q_arr, k_arr, v_arr)


def _split_bf16(a):
    hi = a.astype(BF16)
    lo = (a - hi.astype(F32)).astype(BF16)
    return hi, lo


def _mix_kernel(ona_ref, o1_ref, o2_ref, o3_ref, l1_ref, l2_ref, l3_ref, x_ref,
                gna_ref, gdil_ref, wout_ref, g2_ref, wr_hi_ref, wr_lo_ref, br_ref,
                h_ref, hn_ref, idx_ref, gate_ref):
    l1, l2, l3 = l1_ref[...], l2_ref[...], l3_ref[...]
    m = jnp.maximum(jnp.maximum(l1, l2), l3)
    e1, e2, e3 = jnp.exp(l1 - m), jnp.exp(l2 - m), jnp.exp(l3 - m)
    odil = (e1 * o1_ref[...].astype(F32) + e2 * o2_ref[...].astype(F32)
            + e3 * o3_ref[...].astype(F32)) / (e1 + e2 + e3)
    ona = ona_ref[...].astype(F32)

    def gnorm(a, g):
        return a * lax.rsqrt(jnp.mean(a * a, axis=-1, keepdims=True) + RMS_EPS) * g

    mixed = jnp.concatenate([gnorm(ona, gna_ref[...]), gnorm(odil, gdil_ref[...])], axis=-1)
    h = x_ref[...] + jnp.dot(mixed.astype(BF16), wout_ref[...], preferred_element_type=F32)
    h_ref[...] = h
    hn = gnorm(h, g2_ref[...])
    hn_ref[...] = hn

    hi, lo = _split_bf16(hn)
    logits = (jnp.dot(hi, wr_hi_ref[...], preferred_element_type=F32)
              + jnp.dot(hi, wr_lo_ref[...], preferred_element_type=F32)
              + jnp.dot(lo, wr_hi_ref[...], preferred_element_type=F32)) + br_ref[...]

    n_e = logits.shape[-1]
    eidx = lax.broadcasted_iota(jnp.int32, logits.shape, 1).astype(F32)
    vals, idxs = [], []
    cur = logits
    for _ in range(TOP_K):
        mx = jnp.max(cur, axis=-1, keepdims=True)
        ix = jnp.min(jnp.where(cur == mx, eidx, float(n_e)), axis=-1, keepdims=True)
        vals.append(mx)
        idxs.append(ix)
        cur = jnp.where(eidx == ix, -jnp.inf, cur)
    es = [jnp.exp(v - vals[0]) for v in vals]
    tot = es[0] + es[1] + es[2] + es[3]
    lane = lax.broadcasted_iota(jnp.int32, idx_ref.shape, 1)
    idx_out = jnp.zeros(idx_ref.shape, F32)
    gate_out = jnp.zeros(gate_ref.shape, F32)
    for k in range(TOP_K):
        idx_out = jnp.where(lane == k, idxs[k], idx_out)
        gate_out = jnp.where(lane == k, es[k] / tot, gate_out)
    idx_ref[...] = idx_out.astype(jnp.int32)
    gate_ref[...] = gate_out


def _mix_outproj_router(ona, o_br, l_br, x2, gna, gdil, w_out, g2, w_router, b_router):
    n, d = x2.shape
    tm = TOKEN_TILE
    n_e = w_router.shape[1]
    wr_hi, wr_lo = _split_bf16(w_router)
    row = lambda w: pl.BlockSpec((tm, w), lambda i: (i, 0))
    full = lambda a: pl.BlockSpec(a.shape, lambda i: (0,) * a.ndim)
    gna2, gdil2, g22, br2 = gna.reshape(1, -1), gdil.reshape(1, -1), g2.reshape(1, -1), b_router.reshape(1, -1)
    wout_b = w_out.astype(BF16)
    return pl.pallas_call(
        _mix_kernel,
        out_shape=(jax.ShapeDtypeStruct((n, d), F32), jax.ShapeDtypeStruct((n, d), F32),
                   jax.ShapeDtypeStruct((n, LANES), jnp.int32), jax.ShapeDtypeStruct((n, LANES), F32)),
        grid=(n // tm,),
        in_specs=[row(MIX_NA), row(MIX_DIL), row(MIX_DIL), row(MIX_DIL),
                  row(MIX_DIL), row(MIX_DIL), row(MIX_DIL), row(d),
                  full(gna2), full(gdil2), full(wout_b), full(g22), full(wr_hi), full(wr_lo), full(br2)],
        out_specs=(row(d), row(d), row(LANES), row(LANES)),
        compiler_params=_cparams("arbitrary"),
        name="mix_outproj_router",
    )(ona, *o_br, *l_br, x2, gna2, gdil2, wout_b, g22, wr_hi, wr_lo, br2)


def _expert_kernel(be_ref, nv_ref, tok_ref, dst_ref, hn_hbm, wg_ref, bg_ref, wu_ref, bu_ref, wd_ref, bd_ref,
                   ys_hbm, xbuf, ybuf, sem):
    i = pl.program_id(0)

    def row_in(j):
        return pltpu.make_async_copy(hn_hbm.at[pl.ds(tok_ref[0, 0, j], 1), :], xbuf.at[pl.ds(j, 1), :], sem.at[0])

    def row_out(j):
        return pltpu.make_async_copy(ybuf.at[pl.ds(j, 1), :], ys_hbm.at[pl.ds(dst_ref[0, 0, j], 1), :], sem.at[1])

    nv = nv_ref[i]

    @pl.when(i == 0)
    def _():
        xbuf[...] = jnp.zeros_like(xbuf)

    @pl.when(nv > 0)
    def _():
        def start_in(j, c):
            row_in(j).start()
            return c

        def wait_in(j, c):
            row_in(j).wait()
            return c

        lax.fori_loop(0, nv, start_in, 0)
        lax.fori_loop(0, nv, wait_in, 0)
        x = xbuf[...].astype(BF16)
        g = jnp.dot(x, wg_ref[0], preferred_element_type=F32) + bg_ref[0]
        u = jnp.dot(x, wu_ref[0], preferred_element_type=F32) + bu_ref[0]
        g = jnp.minimum(g, SWIGLU_LIMIT)
        u = jnp.clip(u, -SWIGLU_LIMIT, SWIGLU_LIMIT)
        hmid = (u + 1.0) * (g * jax.nn.sigmoid(SWIGLU_ALPHA * g))
        ybuf[...] = jnp.dot(hmid.astype(BF16), wd_ref[0], preferred_element_type=F32) + bd_ref[0]

        def start_out(j, c):
            row_out(j).start()
            return c

        def wait_out(j, c):
            row_out(j).wait()
            return c

        lax.fori_loop(0, nv, start_out, 0)
        lax.fori_loop(0, nv, wait_out, 0)


def _experts(hn, blk_e, blk_nv, tok_idx, dst_idx, w_gate, b_gate, w_up, b_up, w_down, b_down, n_rows_out):
    n, d = hn.shape
    n_e, _, dff = w_gate.shape
    n_blocks = blk_e.shape[0]
    blk = EXPERT_BLOCK
    wspec = lambda a, b: pl.BlockSpec((1, a, b), lambda i, be, nv: (be[i], 0, 0))
    ispec = pl.BlockSpec((1, 1, blk), lambda i, be, nv: (i, 0, 0), memory_space=pltpu.SMEM)
    return pl.pallas_call(
        _expert_kernel,
        out_shape=jax.ShapeDtypeStruct((n_rows_out, d), F32),
        grid_spec=pltpu.PrefetchScalarGridSpec(
            num_scalar_prefetch=2,
            grid=(n_blocks,),
            in_specs=[ispec, ispec, pl.BlockSpec(memory_space=pl.ANY),
                      wspec(d, dff), wspec(1, dff), wspec(d, dff), wspec(1, dff), wspec(dff, d), wspec(1, d)],
            out_specs=pl.BlockSpec(memory_space=pl.ANY),
            scratch_shapes=[pltpu.VMEM((blk, d), F32), pltpu.VMEM((blk, d), F32),
                            pltpu.SemaphoreType.DMA((2,))],
        ),
        compiler_params=_cparams("arbitrary"),
        name="experts",
    )(blk_e, blk_nv, tok_idx, dst_idx, hn,
      w_gate.astype(BF16), b_gate.reshape(n_e, 1, dff), w_up.astype(BF16), b_up.reshape(n_e, 1, dff),
      w_down.astype(BF16), b_down.reshape(n_e, 1, d))


def _route(top_idx, n):
    blk = EXPERT_BLOCK
    n4 = n * TOP_K
    n_blocks = n4 // blk + N_EXPERTS
    flat_e = top_idx.reshape(-1)
    skeys = jnp.sort(flat_e * n4 + jnp.arange(n4, dtype=jnp.int32))
    s_flat = skeys % n4
    counts = jnp.sum((flat_e[:, None] == jnp.arange(N_EXPERTS, dtype=jnp.int32)[None, :]).astype(jnp.int32), axis=0)
    starts = jnp.cumsum(counts) - counts
    pcounts = (counts + blk - 1) // blk * blk
    pends = jnp.cumsum(pcounts)
    pstarts = pends - pcounts
    b0 = jnp.arange(n_blocks, dtype=jnp.int32) * blk
    blk_e = jnp.clip(jnp.searchsorted(pends, b0, side="right"), 0, N_EXPERTS - 1).astype(jnp.int32)
    boff = b0 - pstarts[blk_e]
    blk_nv = jnp.clip(counts[blk_e] - boff, 0, blk).astype(jnp.int32)
    j = jnp.arange(blk, dtype=jnp.int32)[None, :]
    valid = j < blk_nv[:, None]
    src = jnp.clip((starts[blk_e] + boff)[:, None] + j, 0, n4 - 1)
    flat = s_flat[src]
    tok = jnp.where(valid, flat // TOP_K, 0).astype(jnp.int32)
    dst = jnp.where(valid, (flat % TOP_K) * n + flat // TOP_K, 0).astype(jnp.int32)
    return blk_e, blk_nv, tok.reshape(n_blocks, 1, blk), dst.reshape(n_blocks, 1, blk)


def _combine_kernel(h_ref, gate_ref, y0_ref, y1_ref, y2_ref, y3_ref, g_ref, o_ref):
    gates = gate_ref[...]
    y = h_ref[...]
    for k, y_ref in enumerate((y0_ref, y1_ref, y2_ref, y3_ref)):
        y = y + gates[:, k:k + 1] * y_ref[...]
    o_ref[...] = y * lax.rsqrt(jnp.mean(y * y, axis=-1, keepdims=True) + RMS_EPS) * g_ref[...]


def _combine(h, gates, ys, final_g):
    n, d = h.shape
    tm = TOKEN_TILE
    nt = n // tm
    yspec = lambda k: pl.BlockSpec((tm, d), lambda i, k=k: (k * nt + i, 0))
    return pl.pallas_call(
        _combine_kernel,
        out_shape=jax.ShapeDtypeStruct((n, d), F32),
        grid=(nt,),
        in_specs=[pl.BlockSpec((tm, d), lambda i: (i, 0)), pl.BlockSpec((tm, LANES), lambda i: (i, 0)),
                  yspec(0), yspec(1), yspec(2), yspec(3), pl.BlockSpec((1, d), lambda i: (0, 0))],
        out_specs=pl.BlockSpec((tm, d), lambda i: (i, 0)),
        compiler_params=_cparams("arbitrary"),
        name="combine",
    )(h, gates, ys, ys, ys, ys, final_g.reshape(1, d))


def _layer(x, norm1_g, w_in, rpb, gnorm_na, gnorm_dil, w_out, norm2_g,
           w_router, b_router, w_gate, b_gate, w_up, b_up, w_down, b_down, final_g):
    b, t, d = x.shape
    n = b * t
    x2 = x.reshape(n, d)
    qkv = _qkv_proj(x2, norm1_g, w_in)
    qkv3 = qkv.reshape(b, t, 3 * MIX_WIDTH)

    ona = _na_attention(qkv3, _na_bias_table(rpb)).reshape(n, MIX_NA)

    n_col = MIX_WIDTH // MIX_DIL
    o_br, l_br = [], []
    for window, dil in DIL_PATTERNS:
        half = window // (2 * dil)
        s_len = t // dil
        if dil == 1:
            o, l = _band_attention(qkv3, qkv3, qkv3, (1, 1 + n_col, 1 + 2 * n_col), s_len, 1, dil, half)
        else:
            dq = jnp.concatenate([qkv3[..., MIX_NA:MIX_WIDTH],
                                  qkv3[..., MIX_WIDTH + MIX_NA:2 * MIX_WIDTH],
                                  qkv3[..., 2 * MIX_WIDTH + MIX_NA:]], axis=-1)
            dq = dq.reshape(b, s_len, dil, 3 * MIX_DIL).transpose(0, 2, 1, 3).reshape(b * dil, s_len, 3 * MIX_DIL)
            o, l = _band_attention(dq, dq, dq, (0, 1, 2), s_len, dil, dil, half)
            unperm = lambda a: a.reshape(b, dil, s_len, MIX_DIL).transpose(0, 2, 1, 3)
            o, l = unperm(o), unperm(l)
        o_br.append(o.reshape(n, MIX_DIL))
        l_br.append(l.reshape(n, MIX_DIL))

    h, hn, idx_pad, gate_pad = _mix_outproj_router(ona, o_br, l_br, x2, gnorm_na, gnorm_dil, w_out,
                                                   norm2_g, w_router, b_router)
    blk_e, blk_nv, tok_idx, dst_idx = _route(idx_pad[:, :TOP_K], n)
    ys = _experts(hn, blk_e, blk_nv, tok_idx, dst_idx, w_gate, b_gate, w_up, b_up, w_down, b_down,
                  n * TOP_K)
    return _combine(h, gate_pad, ys, final_g).reshape(b, t, d)


def kernel(x_prompt, x_sample, norm1_g, w_in, rpb, gnorm_na, gnorm_dil, w_out, norm2_g, w_router, b_router,
           w_gate, b_gate, w_up, b_up, w_down, b_down, final_g):
    bp = x_prompt.shape[0]
    x = jnp.concatenate([x_prompt, x_sample], axis=0)
    y = _layer(x, norm1_g[0], w_in[0], rpb[0], gnorm_na[0], gnorm_dil[0], w_out[0], norm2_g[0],
               w_router[0], b_router[0], w_gate[0], b_gate[0], w_up[0], b_up[0], w_down[0], b_down[0], final_g)
    return (y[:bp], y[bp:])
```

```python
import functools

import jax
import jax.numpy as jnp
from jax import lax
from jax.experimental import pallas as pl
from jax.experimental.pallas import tpu as pltpu

F32 = jnp.float32
BF16 = jnp.bfloat16

HEAD_DIM = 64
N_HEADS_NA = 8
N_HEADS_DIL = 8
MIX_NA = N_HEADS_NA * HEAD_DIM
MIX_DIL = N_HEADS_DIL * HEAD_DIM
MIX_WIDTH = MIX_NA + MIX_DIL
GRID_W = 64
NA_ROWS = 8
NA_COLS = 16
DIL_PATTERNS = ((128, 1), (512, 4), (2048, 16))
N_EXPERTS = 32
TOP_K = 4
SWIGLU_LIMIT = 7.0
SWIGLU_ALPHA = 1.702
RMS_EPS = 1e-5

LANES = 128
NEG_BIG = -1e30
VMEM_LIMIT = 56 * 1024 * 1024
TOKEN_TILE = 512
EXPERT_BLOCK = 256
EXPERT_CHUNK = 256
SUBLANES = 8


def _cparams(*sem):
    return pltpu.CompilerParams(dimension_semantics=sem, vmem_limit_bytes=VMEM_LIMIT)


def _store_row_tiles(ref, lead, val):
    rows = val.shape[0]
    for s in range(val.shape[1] // LANES):
        ref[lead + (pl.ds(s, rows, stride=SUBLANES), slice(None))] = val[:, s * LANES:(s + 1) * LANES]


def _load_row_tiles(ref, lead, rows, dtype):
    n_s = SUBLANES
    return jnp.concatenate(
        [ref[lead + (pl.ds(s, rows, stride=SUBLANES), slice(None))].astype(dtype) for s in range(n_s)], axis=-1)


def _qkv_kernel(x_ref, g_ref, w_ref, o_ref):
    x = x_ref[...]
    ms = jnp.mean(x * x, axis=-1, keepdims=True)
    xn = (x * lax.rsqrt(ms + RMS_EPS) * g_ref[...]).astype(BF16)
    n_out = o_ref.shape[1]
    chunk = 512
    for j in range(n_out // chunk):
        acc = jnp.dot(xn, w_ref[:, j * chunk:(j + 1) * chunk], preferred_element_type=F32)
        if (j + 1) * chunk <= MIX_WIDTH:
            acc = acc * (HEAD_DIM ** -0.5)
        o_ref[:, j * chunk:(j + 1) * chunk] = acc.astype(BF16)


def _qkv_proj(x2, g1, w_in):
    n, d = x2.shape
    n_out = w_in.shape[1]
    return pl.pallas_call(
        _qkv_kernel,
        out_shape=jax.ShapeDtypeStruct((n, n_out), BF16),
        grid=(n // TOKEN_TILE,),
        in_specs=[
            pl.BlockSpec((TOKEN_TILE, d), lambda i: (i, 0)),
            pl.BlockSpec((1, d), lambda i: (0, 0)),
            pl.BlockSpec((d, n_out), lambda i: (0, 0)),
        ],
        out_specs=pl.BlockSpec((TOKEN_TILE, n_out), lambda i: (i, 0)),
        compiler_params=_cparams("arbitrary"),
        name="qkv_proj",
    )(x2, g1.reshape(1, d), w_in.astype(BF16))


def _na_bias_table(rpb):
    kw = NA_COLS
    cols = jnp.arange(GRID_W)
    col_start = jnp.clip(cols - kw // 2, 0, GRID_W - kw)
    col_ok = (cols[None, :] >= col_start[:, None]) & (cols[None, :] < col_start[:, None] + kw)
    dc_idx = jnp.clip(cols[None, :] - cols[:, None] + (kw - 1), 0, 2 * kw - 2)
    b = rpb.astype(F32)[:, :, dc_idx]
    b = jnp.where(col_ok[None, None], b, NEG_BIG)
    slabs = [b[:, o:o + NA_ROWS] for o in range(NA_ROWS)]
    t = jnp.stack(slabs, 0)
    t = jnp.transpose(t, (0, 1, 3, 2, 4))
    return t.reshape(NA_ROWS, rpb.shape[0], GRID_W, NA_ROWS * GRID_W)


def _softmax_pv(s, v):
    m = jnp.max(s, axis=-1, keepdims=True)
    p = jnp.exp(s - m)
    l = jnp.sum(p, axis=-1, keepdims=True)
    o = jnp.dot(p.astype(BF16), v, preferred_element_type=F32)
    return o, m, l


def _na_kernel(q_ref, k_ref, v_ref, b_ref, o_ref):
    t = q_ref.shape[1]
    n_rows = t // GRID_W
    kh = NA_ROWS
    lane = lax.broadcasted_iota(jnp.int32, (GRID_W, LANES), 1)
    head0 = lane < HEAD_DIM
    for p in range(MIX_NA // LANES):
        cs = slice(p * LANES, (p + 1) * LANES)

        def row_body(r, carry, p=p, cs=cs):
            rs = jnp.clip(r - kh // 2, 0, n_rows - kh)
            off = rs - r + (NA_ROWS - 1)
            q0 = pl.multiple_of(r * GRID_W, GRID_W)
            k0 = pl.multiple_of(rs * GRID_W, GRID_W)
            q = q_ref[0, pl.ds(q0, GRID_W), cs]
            kwin = k_ref[0, pl.ds(k0, kh * GRID_W), cs]
            vwin = v_ref[0, pl.ds(k0, kh * GRID_W), cs]
            outs = []
            for h in range(2):
                qh = jnp.where(head0 if h == 0 else ~head0, q, jnp.zeros_like(q))
                s = lax.dot_general(qh, kwin, (((1,), (1,)), ((), ())), preferred_element_type=F32)
                s = s + b_ref[off, 2 * p + h]
                o, _, l = _softmax_pv(s, vwin)
                outs.append(o / l)
            o_ref[0, pl.ds(q0, GRID_W), cs] = jnp.where(head0, outs[0], outs[1]).astype(o_ref.dtype)
            return carry

        lax.fori_loop(0, n_rows, row_body, 0)


def _na_attention(qkv3, bias_tab):
    b, t, _ = qkv3.shape
    qblk = MIX_WIDTH // MIX_NA
    spec = lambda c: pl.BlockSpec((1, t, MIX_NA), lambda i, c=c: (i, 0, c))
    return pl.pallas_call(
        _na_kernel,
        out_shape=jax.ShapeDtypeStruct((b, t, MIX_NA), BF16),
        grid=(b,),
        in_specs=[spec(0), spec(qblk), spec(2 * qblk),
                  pl.BlockSpec(bias_tab.shape, lambda i: (0, 0, 0, 0))],
        out_specs=pl.BlockSpec((1, t, MIX_NA), lambda i: (i, 0, 0)),
        compiler_params=_cparams("arbitrary"),
        name="na_attention",
    )(qkv3, qkv3, qkv3, bias_tab)


def _band_kernel(q_ref, k_ref, v_ref, o_ref, lse_ref, *, dil, half, qb, kwin):
    gs, s_len, _ = q_ref.shape
    n_qb = s_len // qb
    row = lax.broadcasted_iota(jnp.int32, (qb, kwin), 0)
    col = lax.broadcasted_iota(jnp.int32, (qb, kwin), 1)
    lane = lax.broadcasted_iota(jnp.int32, (qb, LANES), 1)
    head0 = lane < HEAD_DIM
    for p in range(MIX_DIL // LANES):
        cs = slice(p * LANES, (p + 1) * LANES)

        def body(it, carry, p=p, cs=cs):
            g = it // n_qb
            qs = (it % n_qb) * qb
            ks = jnp.clip(qs - half, 0, s_len - kwin)
            qs = pl.multiple_of(qs, qb)
            ks = pl.multiple_of(ks, half)
            q = q_ref[g, pl.ds(qs, qb), cs]
            kw = k_ref[g, pl.ds(ks, kwin), cs]
            vw = v_ref[g, pl.ds(ks, kwin), cs]
            dist = jnp.abs(col - row + (ks - qs))
            ok = dist <= half
            fdist = dist.astype(F32)
            outs, lses = [], []
            for h in range(2):
                slope = 2.0 ** (-8.0 * (2 * p + h + 1) / N_HEADS_DIL) * dil
                qh = jnp.where(head0 if h == 0 else ~head0, q, jnp.zeros_like(q))
                s = lax.dot_general(qh, kw, (((1,), (1,)), ((), ())), preferred_element_type=F32)
                s = jnp.where(ok, s - slope * fdist, NEG_BIG)
                o, m, l = _softmax_pv(s, vw)
                outs.append(o / l)
                lses.append(m + jnp.log(l))
            o_ref[g, pl.ds(qs, qb), cs] = jnp.where(head0, outs[0], outs[1]).astype(o_ref.dtype)
            lse_ref[g, pl.ds(qs, qb), cs] = jnp.where(head0, lses[0], lses[1])
            return carry

        lax.fori_loop(0, gs * n_qb, body, 0)


def _band_attention(q_arr, k_arr, v_arr, cols, s_len, gs, dil, half):
    g = q_arr.shape[0]
    qb = min(128, s_len)
    kwin = min(qb + 2 * half, s_len)
    spec = lambda c: pl.BlockSpec((gs, s_len, MIX_DIL), lambda i, c=c: (i, 0, c))
    ospec = pl.BlockSpec((gs, s_len, MIX_DIL), lambda i: (i, 0, 0))
    return pl.pallas_call(
        functools.partial(_band_kernel, dil=dil, half=half, qb=qb, kwin=kwin),
        out_shape=(jax.ShapeDtypeStruct((g, s_len, MIX_DIL), BF16),
                   jax.ShapeDtypeStruct((g, s_len, MIX_DIL), F32)),
        grid=(g // gs,),
        in_specs=[spec(cols[0]), spec(cols[1]), spec(cols[2])],
        out_specs=(ospec, ospec),
        compiler_params=_cparams("arbitrary"),
        name=f"band_attention_d{dil}",
    )(q_arr, k_arr, v_arr)


def _split_bf16(a):
    hi = a.astype(BF16)
    lo = (a - hi.astype(F32)).astype(BF16)
    return hi, lo


def _mix_kernel(ona_ref, o1_ref, o2_ref, o3_ref, l1_ref, l2_ref, l3_ref, x_ref,
                gna_ref, gdil_ref, wout_ref, g2_ref, wr_hi_ref, wr_lo_ref, br_ref,
                h_ref, hn_ref, idx_ref, gate_ref):
    l1, l2, l3 = l1_ref[...], l2_ref[...], l3_ref[...]
    m = jnp.maximum(jnp.maximum(l1, l2), l3)
    e1, e2, e3 = jnp.exp(l1 - m), jnp.exp(l2 - m), jnp.exp(l3 - m)
    odil = (e1 * o1_ref[...].astype(F32) + e2 * o2_ref[...].astype(F32)
            + e3 * o3_ref[...].astype(F32)) / (e1 + e2 + e3)
    ona = ona_ref[...].astype(F32)

    def gnorm(a, g):
        return a * lax.rsqrt(jnp.mean(a * a, axis=-1, keepdims=True) + RMS_EPS) * g

    mixed = jnp.concatenate([gnorm(ona, gna_ref[...]), gnorm(odil, gdil_ref[...])], axis=-1)
    h = x_ref[...] + jnp.dot(mixed.astype(BF16), wout_ref[...], preferred_element_type=F32)
    h_ref[...] = h
    hn = gnorm(h, g2_ref[...])
    _store_row_tiles(hn_ref, (), hn)

    hi, lo = _split_bf16(hn)
    logits = (jnp.dot(hi, wr_hi_ref[...], preferred_element_type=F32)
              + jnp.dot(hi, wr_lo_ref[...], preferred_element_type=F32)
              + jnp.dot(lo, wr_hi_ref[...], preferred_element_type=F32)) + br_ref[...]

    n_e = logits.shape[-1]
    eidx = lax.broadcasted_iota(jnp.int32, logits.shape, 1).astype(F32)
    vals, idxs = [], []
    cur = logits
    for _ in range(TOP_K):
        mx = jnp.max(cur, axis=-1, keepdims=True)
        ix = jnp.min(jnp.where(cur == mx, eidx, float(n_e)), axis=-1, keepdims=True)
        vals.append(mx)
        idxs.append(ix)
        cur = jnp.where(eidx == ix, -jnp.inf, cur)
    es = [jnp.exp(v - vals[0]) for v in vals]
    tot = es[0] + es[1] + es[2] + es[3]
    lane = lax.broadcasted_iota(jnp.int32, idx_ref.shape, 1)
    idx_out = jnp.zeros(idx_ref.shape, F32)
    gate_out = jnp.zeros(gate_ref.shape, F32)
    for k in range(TOP_K):
        idx_out = jnp.where(lane == k, idxs[k], idx_out)
        gate_out = jnp.where(lane == k, es[k] / tot, gate_out)
    idx_ref[...] = idx_out.astype(jnp.int32)
    gate_ref[...] = gate_out


def _mix_outproj_router(ona, o_br, l_br, x2, gna, gdil, w_out, g2, w_router, b_router):
    n, d = x2.shape
    tm = TOKEN_TILE
    n_e = w_router.shape[1]
    wr_hi, wr_lo = _split_bf16(w_router)
    row = lambda w: pl.BlockSpec((tm, w), lambda i: (i, 0))
    full = lambda a: pl.BlockSpec(a.shape, lambda i: (0,) * a.ndim)
    gna2, gdil2, g22, br2 = gna.reshape(1, -1), gdil.reshape(1, -1), g2.reshape(1, -1), b_router.reshape(1, -1)
    wout_b = w_out.astype(BF16)
    return pl.pallas_call(
        _mix_kernel,
        out_shape=(jax.ShapeDtypeStruct((n, d), F32), jax.ShapeDtypeStruct((n * SUBLANES, LANES), F32),
                   jax.ShapeDtypeStruct((n, LANES), jnp.int32), jax.ShapeDtypeStruct((n, LANES), F32)),
        grid=(n // tm,),
        in_specs=[row(MIX_NA), row(MIX_DIL), row(MIX_DIL), row(MIX_DIL),
                  row(MIX_DIL), row(MIX_DIL), row(MIX_DIL), row(d),
                  full(gna2), full(gdil2), full(wout_b), full(g22), full(wr_hi), full(wr_lo), full(br2)],
        out_specs=(row(d), pl.BlockSpec((tm * SUBLANES, LANES), lambda i: (i, 0)), row(LANES), row(LANES)),
        compiler_params=_cparams("arbitrary"),
        name="mix_outproj_router",
    )(ona, *o_br, *l_br, x2, gna2, gdil2, wout_b, g22, wr_hi, wr_lo, br2)


def _expert_kernel(be_ref, tokc_ref, tokn_ref, dstp_ref, hn_hbm, wg_ref, bg_ref, wu_ref, bu_ref, wd_ref, bd_ref,
                   ys_hbm, xbuf, ybuf, hbuf, gsem, ssem):
    del be_ref
    s = pl.program_id(0)
    last = pl.num_programs(0) - 1
    slot = s % 2
    nslot = 1 - slot
    blk, dff = hbuf.shape
    tile = SUBLANES

    def gather_row(tok_ref, j, sl):
        src = pl.multiple_of(tok_ref[0, 0, j], tile)
        return pltpu.make_async_copy(hn_hbm.at[pl.ds(src, tile), :], xbuf.at[sl, pl.ds(j * tile, tile), :],
                                     gsem.at[sl])

    def scatter_row(j, sl):
        dst = pl.multiple_of(dstp_ref[0, 0, j], tile)
        return pltpu.make_async_copy(ybuf.at[sl, pl.ds(j * tile, tile), :], ys_hbm.at[pl.ds(dst, tile), :],
                                     ssem.at[sl])

    def gather_block(sl):
        return pltpu.make_async_copy(hn_hbm.at[pl.ds(0, blk * tile), :], xbuf.at[sl], gsem.at[sl])

    def scatter_block(sl):
        return pltpu.make_async_copy(ybuf.at[sl], ys_hbm.at[pl.ds(0, blk * tile), :], ssem.at[sl])

    @pl.when(s == 0)
    def _():
        ybuf[...] = jnp.zeros_like(ybuf)

        def first(j, c):
            gather_row(tokc_ref, j, 0).start()
            return c

        lax.fori_loop(0, blk, first, 0)

    gather_block(slot).wait()

    @pl.when(s >= 1)
    def _():
        scatter_block(slot).wait()

    for j in range(blk):
        gather_row(tokn_ref, j, nslot).start()
        scatter_row(j, nslot).start()

    n_chunk = dff // EXPERT_CHUNK
    x = _load_row_tiles(xbuf, (slot,), blk, BF16)
    for c in range(n_chunk):
        cs = slice(c * EXPERT_CHUNK, (c + 1) * EXPERT_CHUNK)
        g = jnp.dot(x, wg_ref[0, :, cs], preferred_element_type=F32) + bg_ref[0, :, cs]
        u = jnp.dot(x, wu_ref[0, :, cs], preferred_element_type=F32) + bu_ref[0, :, cs]
        g = jnp.minimum(g, SWIGLU_LIMIT)
        u = jnp.clip(u, -SWIGLU_LIMIT, SWIGLU_LIMIT)
        hbuf[:, cs] = ((u + 1.0) * (g * jax.nn.sigmoid(SWIGLU_ALPHA * g))).astype(BF16)
    hmid = hbuf[...]
    y = jnp.dot(hmid, wd_ref[0], preferred_element_type=F32) + bd_ref[0]
    _store_row_tiles(ybuf, (slot,), y)

    @pl.when(s == last)
    def _():
        gather_block(nslot).wait()
        scatter_block(nslot).wait()


def _experts(hn, blk_e, tok_idx, dst_idx, w_gate, b_gate, w_up, b_up, w_down, b_down, n_rows_out):
    n_e, d, dff = w_gate.shape
    n_steps = blk_e.shape[0]
    blk = EXPERT_BLOCK
    wspec = lambda a, b: pl.BlockSpec((1, a, b), lambda i, be: (be[i], 0, 0))
    ispec = lambda o: pl.BlockSpec((1, 1, blk), lambda i, be, o=o: (i + o, 0, 0), memory_space=pltpu.SMEM)
    return pl.pallas_call(
        _expert_kernel,
        out_shape=jax.ShapeDtypeStruct((n_rows_out * SUBLANES, LANES), F32),
        grid_spec=pltpu.PrefetchScalarGridSpec(
            num_scalar_prefetch=1,
            grid=(n_steps,),
            in_specs=[ispec(0), ispec(1), ispec(0), pl.BlockSpec(memory_space=pl.ANY),
                      wspec(d, dff), wspec(1, dff), wspec(d, dff), wspec(1, dff), wspec(dff, d), wspec(1, d)],
            out_specs=pl.BlockSpec(memory_space=pl.ANY),
            scratch_shapes=[pltpu.VMEM((2, blk * SUBLANES, LANES), F32), pltpu.VMEM((2, blk * SUBLANES, LANES), F32),
                            pltpu.VMEM((blk, dff), BF16),
                            pltpu.SemaphoreType.DMA((2,)), pltpu.SemaphoreType.DMA((2,))],
        ),
        compiler_params=_cparams("arbitrary"),
        name="experts",
    )(blk_e, tok_idx, tok_idx, dst_idx, hn,
      w_gate.astype(BF16), b_gate.reshape(n_e, 1, dff), w_up.astype(BF16), b_up.reshape(n_e, 1, dff),
      w_down.astype(BF16), b_down.reshape(n_e, 1, d))


def _route(top_idx, n):
    blk = EXPERT_BLOCK
    n4 = n * TOP_K
    n_blocks = n4 // blk + N_EXPERTS
    flat_e = top_idx.reshape(-1)
    skeys = jnp.sort(flat_e * n4 + jnp.arange(n4, dtype=jnp.int32))
    s_flat = skeys % n4
    counts = jnp.sum((flat_e[:, None] == jnp.arange(N_EXPERTS, dtype=jnp.int32)[None, :]).astype(jnp.int32), axis=0)
    starts = jnp.cumsum(counts) - counts
    pcounts = (counts + blk - 1) // blk * blk
    pends = jnp.cumsum(pcounts)
    pstarts = pends - pcounts
    b0 = jnp.arange(n_blocks + 2, dtype=jnp.int32) * blk
    blk_e = jnp.clip(jnp.searchsorted(pends, b0, side="right"), 0, N_EXPERTS - 1).astype(jnp.int32)
    boff = b0 - pstarts[blk_e]
    blk_nv = jnp.clip(counts[blk_e] - boff, 0, blk).astype(jnp.int32)
    j = jnp.arange(blk, dtype=jnp.int32)[None, :]
    valid = j < blk_nv[:, None]
    src = jnp.clip((starts[blk_e] + boff)[:, None] + j, 0, n4 - 1)
    flat = s_flat[src]
    tok = (jnp.where(valid, flat // TOP_K, 0) * SUBLANES).astype(jnp.int32)
    dst = (jnp.where(valid, (flat % TOP_K) * n + flat // TOP_K, n4 + j) * SUBLANES).astype(jnp.int32)
    dst_prev = jnp.concatenate([(n4 + j) * SUBLANES, dst[:n_blocks]], axis=0)
    return blk_e[:n_blocks + 1], tok.reshape(n_blocks + 2, 1, blk), dst_prev.reshape(n_blocks + 1, 1, blk)


def _combine_kernel(h_ref, gate_ref, y0_ref, y1_ref, y2_ref, y3_ref, g_ref, o_ref):
    gates = gate_ref[...]
    y = h_ref[...]
    for k, y_ref in enumerate((y0_ref, y1_ref, y2_ref, y3_ref)):
        y = y + gates[:, k:k + 1] * _load_row_tiles(y_ref, (), y.shape[0], F32)
    o_ref[...] = y * lax.rsqrt(jnp.mean(y * y, axis=-1, keepdims=True) + RMS_EPS) * g_ref[...]


def _combine(h, gates, ys, final_g):
    n, d = h.shape
    tm = TOKEN_TILE
    nt = n // tm
    yspec = lambda k: pl.BlockSpec((tm * SUBLANES, LANES), lambda i, k=k: (k * nt + i, 0))
    return pl.pallas_call(
        _combine_kernel,
        out_shape=jax.ShapeDtypeStruct((n, d), F32),
        grid=(nt,),
        in_specs=[pl.BlockSpec((tm, d), lambda i: (i, 0)), pl.BlockSpec((tm, LANES), lambda i: (i, 0)),
                  yspec(0), yspec(1), yspec(2), yspec(3), pl.BlockSpec((1, d), lambda i: (0, 0))],
        out_specs=pl.BlockSpec((tm, d), lambda i: (i, 0)),
        compiler_params=_cparams("arbitrary"),
        name="combine",
    )(h, gates, ys, ys, ys, ys, final_g.reshape(1, d))


def _layer(x, norm1_g, w_in, rpb, gnorm_na, gnorm_dil, w_out, norm2_g,
           w_router, b_router, w_gate, b_gate, w_up, b_up, w_down, b_down, final_g):
    b, t, d = x.shape
    n = b * t
    x2 = x.reshape(n, d)
    qkv = _qkv_proj(x2, norm1_g, w_in)
    qkv3 = qkv.reshape(b, t, 3 * MIX_WIDTH)

    ona = _na_attention(qkv3, _na_bias_table(rpb)).reshape(n, MIX_NA)

    n_col = MIX_WIDTH // MIX_DIL
    o_br, l_br = [], []
    for window, dil in DIL_PATTERNS:
        half = window // (2 * dil)
        s_len = t // dil
        if dil == 1:
            o, l = _band_attention(qkv3, qkv3, qkv3, (1, 1 + n_col, 1 + 2 * n_col), s_len, 1, dil, half)
        else:
            dq = jnp.concatenate([qkv3[..., MIX_NA:MIX_WIDTH],
                                  qkv3[..., MIX_WIDTH + MIX_NA:2 * MIX_WIDTH],
                                  qkv3[..., 2 * MIX_WIDTH + MIX_NA:]], axis=-1)
            dq = dq.reshape(b, s_len, dil, 3 * MIX_DIL).transpose(0, 2, 1, 3).reshape(b * dil, s_len, 3 * MIX_DIL)
            o, l = _band_attention(dq, dq, dq, (0, 1, 2), s_len, dil, dil, half)
            unperm = lambda a: a.reshape(b, dil, s_len, MIX_DIL).transpose(0, 2, 1, 3)
            o, l = unperm(o), unperm(l)
        o_br.append(o.reshape(n, MIX_DIL))
        l_br.append(l.reshape(n, MIX_DIL))

    h, hn, idx_pad, gate_pad = _mix_outproj_router(ona, o_br, l_br, x2, gnorm_na, gnorm_dil, w_out,
                                                   norm2_g, w_router, b_router)
    blk_e, tok_idx, dst_idx = _route(idx_pad[:, :TOP_K], n)
    ys = _experts(hn, blk_e, tok_idx, dst_idx, w_gate, b_gate, w_up, b_up, w_down, b_down,
                  n * TOP_K + EXPERT_BLOCK)
    return _combine(h, gate_pad, ys, final_g).reshape(b, t, d)


def kernel(x_prompt, x_sample, norm1_g, w_in, rpb, gnorm_na, gnorm_dil, w_out, norm2_g, w_router, b_router,
           w_gate, b_gate, w_up, b_up, w_down, b_down, final_g):
    bp = x_prompt.shape[0]
    x = jnp.concatenate([x_prompt, x_sample], axis=0)
    y = _layer(x, norm1_g[0], w_in[0], rpb[0], gnorm_na[0], gnorm_dil[0], w_out[0], norm2_g[0],
               w_router[0], b_router[0], w_gate[0], b_gate[0], w_up[0], b_up[0], w_down[0], b_down[0], final_g)
    return (y[:bp], y[bp:])
```

```python
import functools

import jax
import jax.numpy as jnp
from jax import lax
from jax.experimental import pallas as pl
from jax.experimental.pallas import tpu as pltpu

F32 = jnp.float32
BF16 = jnp.bfloat16

HEAD_DIM = 64
N_HEADS_NA = 8
N_HEADS_DIL = 8
MIX_NA = N_HEADS_NA * HEAD_DIM
MIX_DIL = N_HEADS_DIL * HEAD_DIM
MIX_WIDTH = MIX_NA + MIX_DIL
GRID_W = 64
NA_ROWS = 8
NA_COLS = 16
DIL_PATTERNS = ((128, 1), (512, 4), (2048, 16))
N_EXPERTS = 32
TOP_K = 4
SWIGLU_LIMIT = 7.0
SWIGLU_ALPHA = 1.702
RMS_EPS = 1e-5

LANES = 128
NEG_BIG = -1e30
VMEM_LIMIT = 56 * 1024 * 1024
TOKEN_TILE = 512
EXPERT_BLOCK = 256
EXPERT_CHUNK = 256
SUBLANES = 8
ATTN_BLOCKS = 4
DIL_QB = 128
DIL_MIX_ROWS = 256
DIST_MASKED = 1e30


def _cparams(*sem):
    return pltpu.CompilerParams(dimension_semantics=sem, vmem_limit_bytes=VMEM_LIMIT)


def _store_row_tiles(ref, lead, val):
    rows = val.shape[0]
    for s in range(val.shape[1] // LANES):
        ref[lead + (pl.ds(s, rows, stride=SUBLANES), slice(None))] = val[:, s * LANES:(s + 1) * LANES]


def _load_row_tiles(ref, lead, rows, dtype):
    return jnp.concatenate(
        [ref[lead + (pl.ds(s, rows, stride=SUBLANES), slice(None))].astype(dtype) for s in range(SUBLANES)],
        axis=-1)


def _two_batch_specs(n_first, width):
    first = pl.BlockSpec((TOKEN_TILE, width), lambda i: (jnp.minimum(i, n_first - 1), 0))
    second = pl.BlockSpec((TOKEN_TILE, width), lambda i: (jnp.maximum(i - n_first, 0), 0))
    return first, second


def _qkv_kernel(xp_ref, xs_ref, g_ref, w_ref, o_ref, *, n_first):
    x = jnp.where(pl.program_id(0) < n_first, xp_ref[...], xs_ref[...])
    ms = jnp.mean(x * x, axis=-1, keepdims=True)
    xn = (x * lax.rsqrt(ms + RMS_EPS) * g_ref[...]).astype(BF16)
    n_out = o_ref.shape[1]
    chunk = 512
    for j in range(n_out // chunk):
        acc = jnp.dot(xn, w_ref[:, j * chunk:(j + 1) * chunk], preferred_element_type=F32)
        if (j + 1) * chunk <= MIX_WIDTH:
            acc = acc * (HEAD_DIM ** -0.5)
        o_ref[:, j * chunk:(j + 1) * chunk] = acc.astype(BF16)


def _qkv_proj(xp, xs, g1, w_in):
    d = xp.shape[1]
    n = xp.shape[0] + xs.shape[0]
    n_out = w_in.shape[1]
    n_first = xp.shape[0] // TOKEN_TILE
    return pl.pallas_call(
        functools.partial(_qkv_kernel, n_first=n_first),
        out_shape=jax.ShapeDtypeStruct((n, n_out), BF16),
        grid=(n // TOKEN_TILE,),
        in_specs=[
            *_two_batch_specs(n_first, d),
            pl.BlockSpec((1, d), lambda i: (0, 0)),
            pl.BlockSpec((d, n_out), lambda i: (0, 0)),
        ],
        out_specs=pl.BlockSpec((TOKEN_TILE, n_out), lambda i: (i, 0)),
        compiler_params=_cparams("arbitrary"),
        name="qkv_proj",
    )(xp, xs, g1.reshape(1, d), w_in.astype(BF16))


def _na_bias_table(rpb):
    kw = NA_COLS
    cols = jnp.arange(GRID_W)
    col_start = jnp.clip(cols - kw // 2, 0, GRID_W - kw)
    col_ok = (cols[None, :] >= col_start[:, None]) & (cols[None, :] < col_start[:, None] + kw)
    dc_idx = jnp.clip(cols[None, :] - cols[:, None] + (kw - 1), 0, 2 * kw - 2)
    b = rpb.astype(F32)[:, :, dc_idx]
    b = jnp.where(col_ok[None, None], b, NEG_BIG)
    slabs = [b[:, o:o + NA_ROWS] for o in range(NA_ROWS)]
    t = jnp.stack(slabs, 0)
    t = jnp.transpose(t, (0, 1, 3, 2, 4))
    return t.reshape(NA_ROWS, rpb.shape[0] // 2, 2 * GRID_W, NA_ROWS * GRID_W)


def _attend(qs, ks, vs, add_bias):
    nt = (((1,), (1,)), ((), ()))
    ss = [add_bias(u, lax.dot_general(q, k, nt, preferred_element_type=F32)) for u, (q, k) in enumerate(zip(qs, ks))]
    ms = [jnp.max(s, axis=-1, keepdims=True) for s in ss]
    ps = [jnp.exp(s - m) for s, m in zip(ss, ms)]
    ls = [jnp.sum(p, axis=-1, keepdims=True) for p in ps]
    os = [jnp.dot(p.astype(BF16), v, preferred_element_type=F32) for p, v in zip(ps, vs)]
    return [o / l for o, l in zip(os, ls)], [m + jnp.log(l) for m, l in zip(ms, ls)]


def _stack_heads(q, head0):
    zero = jnp.zeros_like(q)
    return jnp.concatenate([jnp.where(head0, q, zero), jnp.where(head0, zero, q)], axis=0)


def _unstack_heads(a, head0, rows):
    return jnp.where(head0, a[:rows], a[rows:])


def _na_kernel(q_ref, k_ref, v_ref, b_ref, o_ref):
    t = q_ref.shape[1]
    n_rows = t // GRID_W
    kh = NA_ROWS
    head0 = lax.broadcasted_iota(jnp.int32, (GRID_W, LANES), 1) < HEAD_DIM

    def rows_body(it, carry):
        rows = [it * ATTN_BLOCKS + u for u in range(ATTN_BLOCKS)]
        starts = [jnp.clip(r - kh // 2, 0, n_rows - kh) for r in rows]
        offs = [rs - r + (NA_ROWS - 1) for r, rs in zip(rows, starts)]
        q0s = [pl.multiple_of(r * GRID_W, GRID_W) for r in rows]
        k0s = [pl.multiple_of(rs * GRID_W, GRID_W) for rs in starts]
        qs = [_stack_heads(q_ref[0, pl.ds(q0, GRID_W), :], head0) for q0 in q0s]
        ks = [k_ref[0, pl.ds(k0, kh * GRID_W), :] for k0 in k0s]
        vs = [v_ref[0, pl.ds(k0, kh * GRID_W), :] for k0 in k0s]
        outs, _ = _attend(qs, ks, vs, lambda u, s: s + b_ref[offs[u], 0])
        for q0, o2 in zip(q0s, outs):
            o_ref[0, pl.ds(q0, GRID_W), :] = _unstack_heads(o2, head0, GRID_W).astype(o_ref.dtype)
        return carry

    lax.fori_loop(0, n_rows // ATTN_BLOCKS, rows_body, 0)


def _na_attention(qkv3, bias_tab):
    b, t, _ = qkv3.shape
    n_pair = MIX_NA // LANES
    col = MIX_WIDTH // LANES
    spec = lambda c: pl.BlockSpec((1, t, LANES), lambda p, i, c=c: (i, 0, c + p))
    return pl.pallas_call(
        _na_kernel,
        out_shape=jax.ShapeDtypeStruct((b, t, MIX_NA), BF16),
        grid=(n_pair, b),
        in_specs=[spec(0), spec(col), spec(2 * col),
                  pl.BlockSpec((NA_ROWS, 1) + bias_tab.shape[2:], lambda p, i: (0, p, 0, 0))],
        out_specs=pl.BlockSpec((1, t, LANES), lambda p, i: (i, 0, p)),
        compiler_params=_cparams("arbitrary", "arbitrary"),
        name="na_attention",
    )(qkv3, qkv3, qkv3, bias_tab)


def _dil_kernel(q_ref, k_ref, v_ref, o_ref, stage, perm, ob, lb, dist):
    t = q_ref.shape[1]
    p = pl.program_id(0)
    qb = DIL_QB
    half = DIL_PATTERNS[0][0] // (2 * DIL_PATTERNS[0][1])
    kmax = qb + 2 * half
    head0 = lax.broadcasted_iota(jnp.int32, (qb, LANES), 1) < HEAD_DIM

    row = lax.broadcasted_iota(jnp.int32, (2 * qb, kmax), 0)
    col = lax.broadcasted_iota(jnp.int32, (2 * qb, kmax), 1)
    qrow = jnp.where(row < qb, row, row - qb)
    for v, rel in enumerate((0, -half, qb - kmax)):
        d = jnp.abs(col - qrow + rel)
        dist[v] = jnp.where(d <= half, d.astype(F32), DIST_MASKED)

    hrow = lax.broadcasted_iota(jnp.int32, (2 * qb, 1), 0)
    head = 2 * jnp.full((2 * qb, 1), p, jnp.int32) + jnp.where(hrow < qb, 0, 1)
    slope = jnp.exp2((head + 1).astype(F32) * (-8.0 / N_HEADS_DIL))

    srcs = (q_ref, k_ref, v_ref)
    for a in range(3):
        stage[a] = srcs[a][0].astype(F32)
    n_perm = 0
    for bi, (window, dil) in enumerate(DIL_PATTERNS):
        assert window // (2 * dil) == half
        s_len = t // dil
        if dil == 1:
            views = tuple(r.at[0] for r in srcs)
        else:
            for a in range(3):
                for r in range(dil):
                    perm[n_perm, a, pl.ds(r * s_len, s_len), :] = (
                        stage[a, pl.ds(r, s_len, stride=dil), :].astype(BF16))
            views = tuple(perm.at[n_perm, a] for a in range(3))
            n_perm += 1
        qsrc, ksrc, vsrc = views
        n_qb = s_len // qb
        kwin = min(kmax, s_len)
        assert n_qb == 1 or n_qb >= 3
        bslope = slope * float(dil)

        def units(it, carry, bi=bi, dil=dil, s_len=s_len, n_qb=n_qb, kwin=kwin, qsrc=qsrc, ksrc=ksrc, vsrc=vsrc,
                  bslope=bslope):
            qrs, krs, vidx, outrows = [], [], [], []
            for u in range(ATTN_BLOCKS):
                blk = it * ATTN_BLOCKS + u
                if n_qb == 1:
                    g, qs, ks, v = blk, 0, 0, 0
                else:
                    g = blk // n_qb
                    j = blk % n_qb
                    qs = j * qb
                    ks = jnp.clip(qs - half, 0, s_len - kwin)
                    v = jnp.where(j == 0, 0, jnp.where(j == n_qb - 1, 2, 1))
                qr = pl.multiple_of(g * s_len + qs, qb)
                qrs.append(qr)
                krs.append(pl.multiple_of(g * s_len + ks, half))
                vidx.append(v)
                outrows.append(pl.ds(qr, qb) if dil == 1 else pl.ds(qs * dil + g, qb, stride=dil))
            qs_ = [_stack_heads(qsrc[pl.ds(qr, qb), :], head0) for qr in qrs]
            ks_ = [ksrc[pl.ds(kr, kwin), :] for kr in krs]
            vs_ = [vsrc[pl.ds(kr, kwin), :] for kr in krs]
            outs, lses = _attend(qs_, ks_, vs_, lambda u, s: s - bslope * dist[vidx[u], :, :kwin])
            for rows, o2, lse2 in zip(outrows, outs, lses):
                ob[bi, rows, :] = _unstack_heads(o2, head0, qb)
                lb[bi, rows, :] = _unstack_heads(lse2, head0, qb)
            return carry

        assert (dil * n_qb) % ATTN_BLOCKS == 0
        lax.fori_loop(0, dil * n_qb // ATTN_BLOCKS, units, 0)

    def mix(c, carry):
        rows = pl.ds(pl.multiple_of(c * DIL_MIX_ROWS, DIL_MIX_ROWS), DIL_MIX_ROWS)
        l1, l2, l3 = lb[0, rows, :], lb[1, rows, :], lb[2, rows, :]
        m = jnp.maximum(jnp.maximum(l1, l2), l3)
        e1, e2, e3 = jnp.exp(l1 - m), jnp.exp(l2 - m), jnp.exp(l3 - m)
        o = (e1 * ob[0, rows, :] + e2 * ob[1, rows, :] + e3 * ob[2, rows, :]) / (e1 + e2 + e3)
        o_ref[0, rows, :] = o.astype(o_ref.dtype)
        return carry

    lax.fori_loop(0, t // DIL_MIX_ROWS, mix, 0)


def _dilated_attention(qkv3):
    b, t, _ = qkv3.shape
    n_pair = MIX_DIL // LANES
    col = MIX_WIDTH // LANES
    first = MIX_NA // LANES
    spec = lambda c: pl.BlockSpec((1, t, LANES), lambda p, i, c=c: (i, 0, first + c + p))
    n_perm = sum(1 for _, dil in DIL_PATTERNS if dil > 1)
    n_br = len(DIL_PATTERNS)
    kmax = DIL_QB + DIL_PATTERNS[0][0] // DIL_PATTERNS[0][1]
    return pl.pallas_call(
        _dil_kernel,
        out_shape=jax.ShapeDtypeStruct((b, t, MIX_DIL), BF16),
        grid=(n_pair, b),
        in_specs=[spec(0), spec(col), spec(2 * col)],
        out_specs=pl.BlockSpec((1, t, LANES), lambda p, i: (i, 0, p)),
        scratch_shapes=[pltpu.VMEM((3, t, LANES), F32), pltpu.VMEM((n_perm, 3, t, LANES), BF16),
                        pltpu.VMEM((n_br, t, LANES), F32), pltpu.VMEM((n_br, t, LANES), F32),
                        pltpu.VMEM((3, 2 * DIL_QB, kmax), F32)],
        compiler_params=_cparams("arbitrary", "arbitrary"),
        name="dilated_attention",
    )(qkv3, qkv3, qkv3)


def _split_bf16(a):
    hi = a.astype(BF16)
    lo = (a - hi.astype(F32)).astype(BF16)
    return hi, lo


def _mix_kernel(ona_ref, odil_ref, xp_ref, xs_ref, gna_ref, gdil_ref, wout_ref, g2_ref, wr_hi_ref, wr_lo_ref,
                br_ref, h_ref, hn_ref, idx_ref, gate_ref, *, n_first):
    ona = ona_ref[...].astype(F32)
    odil = odil_ref[...].astype(F32)
    x = jnp.where(pl.program_id(0) < n_first, xp_ref[...], xs_ref[...])

    def gnorm(a, g):
        return a * lax.rsqrt(jnp.mean(a * a, axis=-1, keepdims=True) + RMS_EPS) * g

    mixed = jnp.concatenate([gnorm(ona, gna_ref[...]), gnorm(odil, gdil_ref[...])], axis=-1)
    h = x + jnp.dot(mixed.astype(BF16), wout_ref[...], preferred_element_type=F32)
    h_ref[...] = h
    hn = gnorm(h, g2_ref[...])
    _store_row_tiles(hn_ref, (), hn)

    hi, lo = _split_bf16(hn)
    logits = (jnp.dot(hi, wr_hi_ref[...], preferred_element_type=F32)
              + jnp.dot(hi, wr_lo_ref[...], preferred_element_type=F32)
              + jnp.dot(lo, wr_hi_ref[...], preferred_element_type=F32)) + br_ref[...]

    n_e = logits.shape[-1]
    eidx = lax.broadcasted_iota(jnp.int32, logits.shape, 1).astype(F32)
    vals, idxs = [], []
    cur = logits
    for _ in range(TOP_K):
        mx = jnp.max(cur, axis=-1, keepdims=True)
        ix = jnp.min(jnp.where(cur == mx, eidx, float(n_e)), axis=-1, keepdims=True)
        vals.append(mx)
        idxs.append(ix)
        cur = jnp.where(eidx == ix, -jnp.inf, cur)
    es = [jnp.exp(v - vals[0]) for v in vals]
    tot = es[0] + es[1] + es[2] + es[3]
    lane = lax.broadcasted_iota(jnp.int32, idx_ref.shape, 1)
    idx_out = jnp.zeros(idx_ref.shape, F32)
    gate_out = jnp.zeros(gate_ref.shape, F32)
    for k in range(TOP_K):
        idx_out = jnp.where(lane == k, idxs[k], idx_out)
        gate_out = jnp.where(lane == k, es[k] / tot, gate_out)
    idx_ref[...] = idx_out.astype(jnp.int32)
    gate_ref[...] = gate_out


def _mix_outproj_router(ona, odil, xp, xs, gna, gdil, w_out, g2, w_router, b_router):
    d = xp.shape[1]
    n = xp.shape[0] + xs.shape[0]
    tm = TOKEN_TILE
    n_first = xp.shape[0] // tm
    wr_hi, wr_lo = _split_bf16(w_router)
    row = lambda w: pl.BlockSpec((tm, w), lambda i: (i, 0))
    full = lambda a: pl.BlockSpec(a.shape, lambda i: (0,) * a.ndim)
    gna2, gdil2, g22, br2 = gna.reshape(1, -1), gdil.reshape(1, -1), g2.reshape(1, -1), b_router.reshape(1, -1)
    wout_b = w_out.astype(BF16)
    return pl.pallas_call(
        functools.partial(_mix_kernel, n_first=n_first),
        out_shape=(jax.ShapeDtypeStruct((n, d), F32), jax.ShapeDtypeStruct((n * SUBLANES, LANES), F32),
                   jax.ShapeDtypeStruct((n, LANES), jnp.int32), jax.ShapeDtypeStruct((n, LANES), F32)),
        grid=(n // tm,),
        in_specs=[row(MIX_NA), row(MIX_DIL), *_two_batch_specs(n_first, d),
                  full(gna2), full(gdil2), full(wout_b), full(g22), full(wr_hi), full(wr_lo), full(br2)],
        out_specs=(row(d), pl.BlockSpec((tm * SUBLANES, LANES), lambda i: (i, 0)), row(LANES), row(LANES)),
        compiler_params=_cparams("arbitrary"),
        name="mix_outproj_router",
    )(ona, odil, xp, xs, gna2, gdil2, wout_b, g22, wr_hi, wr_lo, br2)


def _expert_kernel(be_ref, tokc_ref, tokn_ref, dstp_ref, hn_hbm, wg_ref, bg_ref, wu_ref, bu_ref, wd_ref, bd_ref,
                   ys_hbm, xbuf, ybuf, hbuf, gsem, ssem):
    del be_ref
    s = pl.program_id(0)
    last = pl.num_programs(0) - 1
    slot = s % 2
    nslot = 1 - slot
    blk, dff = hbuf.shape
    tile = SUBLANES

    def gather_row(tok_ref, j, sl):
        src = pl.multiple_of(tok_ref[0, 0, j], tile)
        return pltpu.make_async_copy(hn_hbm.at[pl.ds(src, tile), :], xbuf.at[sl, pl.ds(j * tile, tile), :],
                                     gsem.at[sl])

    def scatter_row(j, sl):
        dst = pl.multiple_of(dstp_ref[0, 0, j], tile)
        return pltpu.make_async_copy(ybuf.at[sl, pl.ds(j * tile, tile), :], ys_hbm.at[pl.ds(dst, tile), :],
                                     ssem.at[sl])

    def gather_block(sl):
        return pltpu.make_async_copy(hn_hbm.at[pl.ds(0, blk * tile), :], xbuf.at[sl], gsem.at[sl])

    def scatter_block(sl):
        return pltpu.make_async_copy(ybuf.at[sl], ys_hbm.at[pl.ds(0, blk * tile), :], ssem.at[sl])

    @pl.when(s == 0)
    def _():
        ybuf[...] = jnp.zeros_like(ybuf)

        def first(j, c):
            gather_row(tokc_ref, j, 0).start()
            return c

        lax.fori_loop(0, blk, first, 0)

    gather_block(slot).wait()

    @pl.when(s >= 1)
    def _():
        scatter_block(slot).wait()

    for j in range(blk):
        gather_row(tokn_ref, j, nslot).start(priority=j % 2)
        scatter_row(j, nslot).start(priority=j % 2)

    n_chunk = dff // EXPERT_CHUNK
    x = _load_row_tiles(xbuf, (slot,), blk, BF16)
    for c in range(n_chunk):
        cs = slice(c * EXPERT_CHUNK, (c + 1) * EXPERT_CHUNK)
        g = jnp.dot(x, wg_ref[0, :, cs], preferred_element_type=F32) + bg_ref[0, :, cs]
        u = jnp.dot(x, wu_ref[0, :, cs], preferred_element_type=F32) + bu_ref[0, :, cs]
        g = jnp.minimum(g, SWIGLU_LIMIT)
        u = jnp.clip(u, -SWIGLU_LIMIT, SWIGLU_LIMIT)
        hbuf[:, cs] = ((u + 1.0) * (g * jax.nn.sigmoid(SWIGLU_ALPHA * g))).astype(BF16)
    hmid = hbuf[...]
    y = jnp.dot(hmid, wd_ref[0], preferred_element_type=F32) + bd_ref[0]
    _store_row_tiles(ybuf, (slot,), y)

    @pl.when(s == last)
    def _():
        gather_block(nslot).wait()
        scatter_block(nslot).wait()


def _experts(hn, blk_e, tok_idx, dst_idx, w_gate, b_gate, w_up, b_up, w_down, b_down, n_rows_out):
    n_e, d, dff = w_gate.shape
    n_steps = blk_e.shape[0]
    blk = EXPERT_BLOCK
    wspec = lambda a, b: pl.BlockSpec((1, a, b), lambda i, be: (be[i], 0, 0))
    ispec = lambda o: pl.BlockSpec((1, 1, blk), lambda i, be, o=o: (i + o, 0, 0), memory_space=pltpu.SMEM)
    return pl.pallas_call(
        _expert_kernel,
        out_shape=jax.ShapeDtypeStruct((n_rows_out * SUBLANES, LANES), F32),
        grid_spec=pltpu.PrefetchScalarGridSpec(
            num_scalar_prefetch=1,
            grid=(n_steps,),
            in_specs=[ispec(0), ispec(1), ispec(0), pl.BlockSpec(memory_space=pl.ANY),
                      wspec(d, dff), wspec(1, dff), wspec(d, dff), wspec(1, dff), wspec(dff, d), wspec(1, d)],
            out_specs=pl.BlockSpec(memory_space=pl.ANY),
            scratch_shapes=[pltpu.VMEM((2, blk * SUBLANES, LANES), F32), pltpu.VMEM((2, blk * SUBLANES, LANES), F32),
                            pltpu.VMEM((blk, dff), BF16),
                            pltpu.SemaphoreType.DMA((2,)), pltpu.SemaphoreType.DMA((2,))],
        ),
        compiler_params=_cparams("arbitrary"),
        name="experts",
    )(blk_e, tok_idx, tok_idx, dst_idx, hn,
      w_gate.astype(BF16), b_gate.reshape(n_e, 1, dff), w_up.astype(BF16), b_up.reshape(n_e, 1, dff),
      w_down.astype(BF16), b_down.reshape(n_e, 1, d))


def _route(top_idx, n):
    blk = EXPERT_BLOCK
    n4 = n * TOP_K
    n_blocks = n4 // blk + N_EXPERTS
    flat_e = top_idx.reshape(-1)
    skeys = jnp.sort(flat_e * n4 + jnp.arange(n4, dtype=jnp.int32))
    s_flat = skeys % n4
    counts = jnp.sum((flat_e[:, None] == jnp.arange(N_EXPERTS, dtype=jnp.int32)[None, :]).astype(jnp.int32), axis=0)
    starts = jnp.cumsum(counts) - counts
    pcounts = (counts + blk - 1) // blk * blk
    pends = jnp.cumsum(pcounts)
    pstarts = pends - pcounts
    b0 = jnp.arange(n_blocks + 2, dtype=jnp.int32) * blk
    blk_e = jnp.minimum(jnp.sum((b0[:, None] >= pends[None, :]).astype(jnp.int32), axis=1), N_EXPERTS - 1)
    boff = b0 - pstarts[blk_e]
    blk_nv = jnp.clip(counts[blk_e] - boff, 0, blk).astype(jnp.int32)
    j = jnp.arange(blk, dtype=jnp.int32)[None, :]
    valid = j < blk_nv[:, None]
    src = jnp.clip((starts[blk_e] + boff)[:, None] + j, 0, n4 - 1)
    flat = s_flat[src]
    tok = (jnp.where(valid, flat // TOP_K, 0) * SUBLANES).astype(jnp.int32)
    dst = (jnp.where(valid, (flat % TOP_K) * n + flat // TOP_K, n4 + j) * SUBLANES).astype(jnp.int32)
    dst_prev = jnp.concatenate([(n4 + j) * SUBLANES, dst[:n_blocks]], axis=0)
    return blk_e[:n_blocks + 1], tok.reshape(n_blocks + 2, 1, blk), dst_prev.reshape(n_blocks + 1, 1, blk)


def _combine_kernel(h_ref, gate_ref, y0_ref, y1_ref, y2_ref, y3_ref, g_ref, op_ref, os_ref, *, n_first):
    gates = gate_ref[...]
    y = h_ref[...]
    for k, y_ref in enumerate((y0_ref, y1_ref, y2_ref, y3_ref)):
        y = y + gates[:, k:k + 1] * _load_row_tiles(y_ref, (), y.shape[0], F32)
    out = y * lax.rsqrt(jnp.mean(y * y, axis=-1, keepdims=True) + RMS_EPS) * g_ref[...]
    i = pl.program_id(0)

    @pl.when(i < n_first)
    def _():
        op_ref[...] = out

    @pl.when(i >= n_first)
    def _():
        os_ref[...] = out


def _combine(h, gates, ys, final_g, n_prompt):
    n, d = h.shape
    tm = TOKEN_TILE
    nt = n // tm
    n_first = n_prompt // tm
    yspec = lambda k: pl.BlockSpec((tm * SUBLANES, LANES), lambda i, k=k: (k * nt + i, 0))
    return pl.pallas_call(
        functools.partial(_combine_kernel, n_first=n_first),
        out_shape=(jax.ShapeDtypeStruct((n_prompt, d), F32), jax.ShapeDtypeStruct((n - n_prompt, d), F32)),
        grid=(nt,),
        in_specs=[pl.BlockSpec((tm, d), lambda i: (i, 0)), pl.BlockSpec((tm, LANES), lambda i: (i, 0)),
                  yspec(0), yspec(1), yspec(2), yspec(3), pl.BlockSpec((1, d), lambda i: (0, 0))],
        out_specs=_two_batch_specs(n_first, d),
        compiler_params=_cparams("arbitrary"),
        name="combine",
    )(h, gates, ys, ys, ys, ys, final_g.reshape(1, d))


def kernel(x_prompt, x_sample, norm1_g, w_in, rpb, gnorm_na, gnorm_dil, w_out, norm2_g, w_router, b_router,
           w_gate, b_gate, w_up, b_up, w_down, b_down, final_g):
    (bp, t, d), bs = x_prompt.shape, x_sample.shape[0]
    b = bp + bs
    n = b * t
    xp, xs = x_prompt.reshape(bp * t, d), x_sample.reshape(bs * t, d)
    qkv3 = _qkv_proj(xp, xs, norm1_g[0], w_in[0]).reshape(b, t, 3 * MIX_WIDTH)
    ona = _na_attention(qkv3, _na_bias_table(rpb[0])).reshape(n, MIX_NA)
    odil = _dilated_attention(qkv3).reshape(n, MIX_DIL)
    h, hn, idx_pad, gate_pad = _mix_outproj_router(ona, odil, xp, xs, gnorm_na[0], gnorm_dil[0], w_out[0],
                                                   norm2_g[0], w_router[0], b_router[0])
    blk_e, tok_idx, dst_idx = _route(idx_pad[:, :TOP_K], n)
    ys = _experts(hn, blk_e, tok_idx, dst_idx, w_gate[0], b_gate[0], w_up[0], b_up[0], w_down[0], b_down[0],
                  n * TOP_K + EXPERT_BLOCK)
    yp, ysm = _combine(h, gate_pad, ys, final_g, bp * t)
    return (yp.reshape(bp, t, d), ysm.reshape(bs, t, d))
```

```python
import functools

import jax
import jax.numpy as jnp
from jax import lax
from jax.experimental import pallas as pl
from jax.experimental.pallas import tpu as pltpu

F32 = jnp.float32
BF16 = jnp.bfloat16

HEAD_DIM = 64
N_HEADS_NA = 8
N_HEADS_DIL = 8
MIX_NA = N_HEADS_NA * HEAD_DIM
MIX_DIL = N_HEADS_DIL * HEAD_DIM
MIX_WIDTH = MIX_NA + MIX_DIL
GRID_W = 64
NA_ROWS = 8
NA_COLS = 16
DIL_PATTERNS = ((128, 1), (512, 4), (2048, 16))
N_EXPERTS = 32
TOP_K = 4
SWIGLU_LIMIT = 7.0
SWIGLU_ALPHA = 1.702
RMS_EPS = 1e-5

LANES = 128
NEG_BIG = -1e30
VMEM_LIMIT = 56 * 1024 * 1024
TOKEN_TILE = 512
EXPERT_BLOCK = 256
EXPERT_CHUNK = 256
SUBLANES = 8
ATTN_BLOCKS = 4
DIL_QB = 128
DIL_MIX_ROWS = 256
DIST_MASKED = 1e30


def _cparams(*sem):
    return pltpu.CompilerParams(dimension_semantics=sem, vmem_limit_bytes=VMEM_LIMIT)


def _store_row_tiles(ref, lead, val, first=0):
    rows = val.shape[0]
    for s in range(val.shape[1] // LANES):
        ref[lead + (pl.ds(first + s, rows, stride=SUBLANES), slice(None))] = val[:, s * LANES:(s + 1) * LANES]


def _load_row_tiles(ref, lead, rows, dtype):
    return jnp.concatenate(
        [ref[lead + (pl.ds(s, rows, stride=SUBLANES), slice(None))].astype(dtype) for s in range(SUBLANES)],
        axis=-1)


def _two_batch_specs(n_first, width):
    first = pl.BlockSpec((TOKEN_TILE, width), lambda i: (jnp.minimum(i, n_first - 1), 0))
    second = pl.BlockSpec((TOKEN_TILE, width), lambda i: (jnp.maximum(i - n_first, 0), 0))
    return first, second


def _qkv_kernel(xp_ref, xs_ref, g_ref, w_ref, o_ref, *, n_first):
    x = jnp.where(pl.program_id(0) < n_first, xp_ref[...], xs_ref[...])
    ms = jnp.mean(x * x, axis=-1, keepdims=True)
    xn = (x * lax.rsqrt(ms + RMS_EPS) * g_ref[...]).astype(BF16)
    n_out = o_ref.shape[1]
    chunk = 512
    for j in range(n_out // chunk):
        acc = jnp.dot(xn, w_ref[:, j * chunk:(j + 1) * chunk], preferred_element_type=F32)
        if (j + 1) * chunk <= MIX_WIDTH:
            acc = acc * (HEAD_DIM ** -0.5)
        o_ref[:, j * chunk:(j + 1) * chunk] = acc.astype(BF16)


def _qkv_proj(xp, xs, g1, w_in):
    d = xp.shape[1]
    n = xp.shape[0] + xs.shape[0]
    n_out = w_in.shape[1]
    n_first = xp.shape[0] // TOKEN_TILE
    return pl.pallas_call(
        functools.partial(_qkv_kernel, n_first=n_first),
        out_shape=jax.ShapeDtypeStruct((n, n_out), BF16),
        grid=(n // TOKEN_TILE,),
        in_specs=[
            *_two_batch_specs(n_first, d),
            pl.BlockSpec((1, d), lambda i: (0, 0)),
            pl.BlockSpec((d, n_out), lambda i: (0, 0)),
        ],
        out_specs=pl.BlockSpec((TOKEN_TILE, n_out), lambda i: (i, 0)),
        compiler_params=_cparams("arbitrary"),
        name="qkv_proj",
    )(xp, xs, g1.reshape(1, d), w_in.astype(BF16))


def _na_bias_table(rpb):
    kw = NA_COLS
    cols = jnp.arange(GRID_W)
    col_start = jnp.clip(cols - kw // 2, 0, GRID_W - kw)
    col_ok = (cols[None, :] >= col_start[:, None]) & (cols[None, :] < col_start[:, None] + kw)
    dc_idx = jnp.clip(cols[None, :] - cols[:, None] + (kw - 1), 0, 2 * kw - 2)
    b = rpb.astype(F32)[:, :, dc_idx]
    b = jnp.where(col_ok[None, None], b, NEG_BIG)
    slabs = [b[:, o:o + NA_ROWS] for o in range(NA_ROWS)]
    t = jnp.stack(slabs, 0)
    t = jnp.transpose(t, (0, 1, 3, 2, 4))
    return t.reshape(NA_ROWS, rpb.shape[0] // 2, 2 * GRID_W, NA_ROWS * GRID_W)


def _attend(qs, ks, vs, add_bias):
    nt = (((1,), (1,)), ((), ()))
    ss = [add_bias(u, lax.dot_general(q, k, nt, preferred_element_type=F32)) for u, (q, k) in enumerate(zip(qs, ks))]
    ms = [jnp.max(s, axis=-1, keepdims=True) for s in ss]
    ps = [jnp.exp(s - m) for s, m in zip(ss, ms)]
    ls = [jnp.sum(p, axis=-1, keepdims=True) for p in ps]
    os = [jnp.dot(p.astype(BF16), v, preferred_element_type=F32) for p, v in zip(ps, vs)]
    return [o / l for o, l in zip(os, ls)], [m + jnp.log(l) for m, l in zip(ms, ls)]


def _stack_heads(q, head0):
    zero = jnp.zeros_like(q)
    return jnp.concatenate([jnp.where(head0, q, zero), jnp.where(head0, zero, q)], axis=0)


def _unstack_heads(a, head0, rows):
    return jnp.where(head0, a[:rows], a[rows:])


def _na_kernel(q_ref, k_ref, v_ref, b_ref, o_ref):
    t = q_ref.shape[1]
    n_rows = t // GRID_W
    kh = NA_ROWS
    head0 = lax.broadcasted_iota(jnp.int32, (GRID_W, LANES), 1) < HEAD_DIM

    def rows_body(it, carry):
        rows = [it * ATTN_BLOCKS + u for u in range(ATTN_BLOCKS)]
        starts = [jnp.clip(r - kh // 2, 0, n_rows - kh) for r in rows]
        offs = [rs - r + (NA_ROWS - 1) for r, rs in zip(rows, starts)]
        q0s = [pl.multiple_of(r * GRID_W, GRID_W) for r in rows]
        k0s = [pl.multiple_of(rs * GRID_W, GRID_W) for rs in starts]
        qs = [_stack_heads(q_ref[0, pl.ds(q0, GRID_W), :], head0) for q0 in q0s]
        ks = [k_ref[0, pl.ds(k0, kh * GRID_W), :] for k0 in k0s]
        vs = [v_ref[0, pl.ds(k0, kh * GRID_W), :] for k0 in k0s]
        outs, _ = _attend(qs, ks, vs, lambda u, s: s + b_ref[offs[u], 0])
        for q0, o2 in zip(q0s, outs):
            o_ref[0, pl.ds(q0, GRID_W), :] = _unstack_heads(o2, head0, GRID_W).astype(o_ref.dtype)
        return carry

    lax.fori_loop(0, n_rows // ATTN_BLOCKS, rows_body, 0)


def _na_attention(qkv3, bias_tab):
    b, t, _ = qkv3.shape
    n_pair = MIX_NA // LANES
    col = MIX_WIDTH // LANES
    spec = lambda c: pl.BlockSpec((1, t, LANES), lambda p, i, c=c: (i, 0, c + p))
    return pl.pallas_call(
        _na_kernel,
        out_shape=jax.ShapeDtypeStruct((b, t, MIX_NA), BF16),
        grid=(n_pair, b),
        in_specs=[spec(0), spec(col), spec(2 * col),
                  pl.BlockSpec((NA_ROWS, 1) + bias_tab.shape[2:], lambda p, i: (0, p, 0, 0))],
        out_specs=pl.BlockSpec((1, t, LANES), lambda p, i: (i, 0, p)),
        compiler_params=_cparams("arbitrary", "arbitrary"),
        name="na_attention",
    )(qkv3, qkv3, qkv3, bias_tab)


def _dil_kernel(q_ref, k_ref, v_ref, o_ref, stage, perm, ob, lb, dist):
    t = q_ref.shape[1]
    p = pl.program_id(0)
    qb = DIL_QB
    half = DIL_PATTERNS[0][0] // (2 * DIL_PATTERNS[0][1])
    kmax = qb + 2 * half
    head0 = lax.broadcasted_iota(jnp.int32, (qb, LANES), 1) < HEAD_DIM

    row = lax.broadcasted_iota(jnp.int32, (2 * qb, kmax), 0)
    col = lax.broadcasted_iota(jnp.int32, (2 * qb, kmax), 1)
    qrow = jnp.where(row < qb, row, row - qb)
    for v, rel in enumerate((0, -half, qb - kmax)):
        d = jnp.abs(col - qrow + rel)
        dist[v] = jnp.where(d <= half, d.astype(F32), DIST_MASKED)

    hrow = lax.broadcasted_iota(jnp.int32, (2 * qb, 1), 0)
    head = 2 * jnp.full((2 * qb, 1), p, jnp.int32) + jnp.where(hrow < qb, 0, 1)
    slope = jnp.exp2((head + 1).astype(F32) * (-8.0 / N_HEADS_DIL))

    srcs = (q_ref, k_ref, v_ref)
    for a in range(3):
        stage[a] = srcs[a][0].astype(F32)
    n_perm = 0
    for bi, (window, dil) in enumerate(DIL_PATTERNS):
        assert window // (2 * dil) == half
        s_len = t // dil
        if dil == 1:
            views = tuple(r.at[0] for r in srcs)
        else:
            for a in range(3):
                for r in range(dil):
                    perm[n_perm, a, pl.ds(r * s_len, s_len), :] = (
                        stage[a, pl.ds(r, s_len, stride=dil), :].astype(BF16))
            views = tuple(perm.at[n_perm, a] for a in range(3))
            n_perm += 1
        qsrc, ksrc, vsrc = views
        n_qb = s_len // qb
        kwin = min(kmax, s_len)
        assert n_qb == 1 or n_qb >= 3
        bslope = slope * float(dil)

        def units(it, carry, bi=bi, dil=dil, s_len=s_len, n_qb=n_qb, kwin=kwin, qsrc=qsrc, ksrc=ksrc, vsrc=vsrc,
                  bslope=bslope):
            qrs, krs, vidx, outrows = [], [], [], []
            for u in range(ATTN_BLOCKS):
                blk = it * ATTN_BLOCKS + u
                if n_qb == 1:
                    g, qs, ks, v = blk, 0, 0, 0
                else:
                    g = blk // n_qb
                    j = blk % n_qb
                    qs = j * qb
                    ks = jnp.clip(qs - half, 0, s_len - kwin)
                    v = jnp.where(j == 0, 0, jnp.where(j == n_qb - 1, 2, 1))
                qr = pl.multiple_of(g * s_len + qs, qb)
                qrs.append(qr)
                krs.append(pl.multiple_of(g * s_len + ks, half))
                vidx.append(v)
                outrows.append(pl.ds(qr, qb) if dil == 1 else pl.ds(qs * dil + g, qb, stride=dil))
            qs_ = [_stack_heads(qsrc[pl.ds(qr, qb), :], head0) for qr in qrs]
            ks_ = [ksrc[pl.ds(kr, kwin), :] for kr in krs]
            vs_ = [vsrc[pl.ds(kr, kwin), :] for kr in krs]
            outs, lses = _attend(qs_, ks_, vs_, lambda u, s: s - bslope * dist[vidx[u], :, :kwin])
            for rows, o2, lse2 in zip(outrows, outs, lses):
                ob[bi, rows, :] = _unstack_heads(o2, head0, qb)
                lb[bi, rows, :] = _unstack_heads(lse2, head0, qb)
            return carry

        assert (dil * n_qb) % ATTN_BLOCKS == 0
        lax.fori_loop(0, dil * n_qb // ATTN_BLOCKS, units, 0)

    def mix(c, carry):
        rows = pl.ds(pl.multiple_of(c * DIL_MIX_ROWS, DIL_MIX_ROWS), DIL_MIX_ROWS)
        l1, l2, l3 = lb[0, rows, :], lb[1, rows, :], lb[2, rows, :]
        m = jnp.maximum(jnp.maximum(l1, l2), l3)
        e1, e2, e3 = jnp.exp(l1 - m), jnp.exp(l2 - m), jnp.exp(l3 - m)
        o = (e1 * ob[0, rows, :] + e2 * ob[1, rows, :] + e3 * ob[2, rows, :]) / (e1 + e2 + e3)
        o_ref[0, rows, :] = o.astype(o_ref.dtype)
        return carry

    lax.fori_loop(0, t // DIL_MIX_ROWS, mix, 0)


def _dilated_attention(qkv3):
    b, t, _ = qkv3.shape
    n_pair = MIX_DIL // LANES
    col = MIX_WIDTH // LANES
    first = MIX_NA // LANES
    spec = lambda c: pl.BlockSpec((1, t, LANES), lambda p, i, c=c: (i, 0, first + c + p))
    n_perm = sum(1 for _, dil in DIL_PATTERNS if dil > 1)
    n_br = len(DIL_PATTERNS)
    kmax = DIL_QB + DIL_PATTERNS[0][0] // DIL_PATTERNS[0][1]
    return pl.pallas_call(
        _dil_kernel,
        out_shape=jax.ShapeDtypeStruct((b, t, MIX_DIL), BF16),
        grid=(n_pair, b),
        in_specs=[spec(0), spec(col), spec(2 * col)],
        out_specs=pl.BlockSpec((1, t, LANES), lambda p, i: (i, 0, p)),
        scratch_shapes=[pltpu.VMEM((3, t, LANES), F32), pltpu.VMEM((n_perm, 3, t, LANES), BF16),
                        pltpu.VMEM((n_br, t, LANES), F32), pltpu.VMEM((n_br, t, LANES), F32),
                        pltpu.VMEM((3, 2 * DIL_QB, kmax), F32)],
        compiler_params=_cparams("arbitrary", "arbitrary"),
        name="dilated_attention",
    )(qkv3, qkv3, qkv3)


def _split_bf16(a):
    hi = a.astype(BF16)
    lo = (a - hi.astype(F32)).astype(BF16)
    return hi, lo


def _mix_kernel(ona_ref, odil_ref, xp_ref, xs_ref, gna_ref, gdil_ref, wout_ref, g2_ref, wr_hi_ref, wr_lo_ref,
                br_ref, h_ref, hn_ref, idx_ref, gate_ref, *, n_first):
    ona = ona_ref[...].astype(F32)
    odil = odil_ref[...].astype(F32)
    x = jnp.where(pl.program_id(0) < n_first, xp_ref[...], xs_ref[...])

    def gnorm(a, g):
        return a * lax.rsqrt(jnp.mean(a * a, axis=-1, keepdims=True) + RMS_EPS) * g

    mixed = jnp.concatenate([gnorm(ona, gna_ref[...]), gnorm(odil, gdil_ref[...])], axis=-1)
    h = x + jnp.dot(mixed.astype(BF16), wout_ref[...], preferred_element_type=F32)
    h_ref[...] = h
    hn = gnorm(h, g2_ref[...])
    _store_row_tiles(hn_ref, (), hn)

    hi, lo = _split_bf16(hn)
    logits = (jnp.dot(hi, wr_hi_ref[...], preferred_element_type=F32)
              + jnp.dot(hi, wr_lo_ref[...], preferred_element_type=F32)
              + jnp.dot(lo, wr_hi_ref[...], preferred_element_type=F32)) + br_ref[...]

    n_e = logits.shape[-1]
    eidx = lax.broadcasted_iota(jnp.int32, logits.shape, 1).astype(F32)
    vals, idxs = [], []
    cur = logits
    for _ in range(TOP_K):
        mx = jnp.max(cur, axis=-1, keepdims=True)
        ix = jnp.min(jnp.where(cur == mx, eidx, float(n_e)), axis=-1, keepdims=True)
        vals.append(mx)
        idxs.append(ix)
        cur = jnp.where(eidx == ix, -jnp.inf, cur)
    es = [jnp.exp(v - vals[0]) for v in vals]
    tot = es[0] + es[1] + es[2] + es[3]
    lane = lax.broadcasted_iota(jnp.int32, idx_ref.shape, 1)
    idx_out = jnp.zeros(idx_ref.shape, F32)
    gate_out = jnp.zeros(gate_ref.shape, F32)
    for k in range(TOP_K):
        idx_out = jnp.where(lane == k, idxs[k], idx_out)
        gate_out = jnp.where(lane == k, es[k] / tot, gate_out)
    idx_ref[...] = idx_out.astype(jnp.int32)
    gate_ref[...] = gate_out


def _mix_outproj_router(ona, odil, xp, xs, gna, gdil, w_out, g2, w_router, b_router):
    d = xp.shape[1]
    n = xp.shape[0] + xs.shape[0]
    tm = TOKEN_TILE
    n_first = xp.shape[0] // tm
    wr_hi, wr_lo = _split_bf16(w_router)
    row = lambda w: pl.BlockSpec((tm, w), lambda i: (i, 0))
    full = lambda a: pl.BlockSpec(a.shape, lambda i: (0,) * a.ndim)
    gna2, gdil2, g22, br2 = gna.reshape(1, -1), gdil.reshape(1, -1), g2.reshape(1, -1), b_router.reshape(1, -1)
    wout_b = w_out.astype(BF16)
    return pl.pallas_call(
        functools.partial(_mix_kernel, n_first=n_first),
        out_shape=(jax.ShapeDtypeStruct((n, d), F32), jax.ShapeDtypeStruct((n * SUBLANES, LANES), F32),
                   jax.ShapeDtypeStruct((n, LANES), jnp.int32), jax.ShapeDtypeStruct((n, LANES), F32)),
        grid=(n // tm,),
        in_specs=[row(MIX_NA), row(MIX_DIL), *_two_batch_specs(n_first, d),
                  full(gna2), full(gdil2), full(wout_b), full(g22), full(wr_hi), full(wr_lo), full(br2)],
        out_specs=(row(d), pl.BlockSpec((tm * SUBLANES, LANES), lambda i: (i, 0)), row(LANES), row(LANES)),
        compiler_params=_cparams("arbitrary"),
        name="mix_outproj_router",
    )(ona, odil, xp, xs, gna2, gdil2, wout_b, g22, wr_hi, wr_lo, br2)


def _expert_kernel(be_ref, tok0_ref, tok1_ref, tok2_ref, dstp_ref, hn_hbm, wg_ref, bg_ref, wu_ref, bu_ref, wd_ref,
                   bd_ref, ys_hbm, xbuf, ybuf, hbuf, wbuf, gsem, ssem):
    s = pl.program_id(0)
    last = pl.num_programs(0) - 1
    xs_cur = s % 3
    xs_new = (s + 2) % 3
    ys_old = (s + 2) % 3
    blk, dff = hbuf.shape
    d = wbuf.shape[2]
    tile = SUBLANES

    def gather_row(tok_ref, j, sl):
        src = pl.multiple_of(tok_ref[0, 0, j], tile)
        return pltpu.make_async_copy(hn_hbm.at[pl.ds(src, tile), :], xbuf.at[sl, pl.ds(j * tile, tile), :],
                                     gsem.at[sl])

    def scatter_row(j, sl, shift=0):
        dst = pl.multiple_of(dstp_ref[0, 0, j] + shift, tile)
        return pltpu.make_async_copy(ybuf.at[sl, pl.ds(j * tile, tile), :], ys_hbm.at[pl.ds(dst, tile), :],
                                     ssem.at[sl])

    def gather_block(sl):
        return pltpu.make_async_copy(hn_hbm.at[pl.ds(0, blk * tile), :], xbuf.at[sl], gsem.at[sl])

    def scatter_block(sl):
        return pltpu.make_async_copy(ybuf.at[sl], ys_hbm.at[pl.ds(0, blk * tile), :], ssem.at[sl])

    @pl.when(s == 0)
    def _():
        ybuf[...] = jnp.zeros_like(ybuf)

        def first(j, c):
            gather_row(tok0_ref, j, 0).start()
            gather_row(tok1_ref, j, 1).start()
            scatter_row(j, 0).start()
            scatter_row(j, 1, shift=-blk * tile).start()
            return c

        lax.fori_loop(0, blk, first, 0)

    gather_block(xs_cur).wait()
    scatter_block(xs_cur).wait()

    @pl.when((s == 0) | (be_ref[s] != be_ref[jnp.maximum(s - 1, 0)]))
    def _():
        for i, w_ref in enumerate((wg_ref, wu_ref, wd_ref)):
            wbuf[i] = w_ref[0].astype(BF16)

    for j in range(blk):
        scatter_row(j, ys_old).start(priority=j % 2)

    x = _load_row_tiles(xbuf, (xs_cur,), blk, BF16)
    for c in range(dff // EXPERT_CHUNK):
        cs = slice(c * EXPERT_CHUNK, (c + 1) * EXPERT_CHUNK)
        g = jnp.dot(x, wbuf[0, :, cs], preferred_element_type=F32) + bg_ref[0, :, cs]
        u = jnp.dot(x, wbuf[1, :, cs], preferred_element_type=F32) + bu_ref[0, :, cs]
        g = jnp.minimum(g, SWIGLU_LIMIT)
        u = jnp.clip(u, -SWIGLU_LIMIT, SWIGLU_LIMIT)
        hbuf[:, cs] = ((u + 1.0) * (g * jax.nn.sigmoid(SWIGLU_ALPHA * g))).astype(BF16)
    for j in range(blk):
        gather_row(tok2_ref, j, xs_new).start(priority=j % 2)
    hmid = hbuf[...]
    probe = pltpu.bitcast(xbuf[xs_cur, 0:2 * SUBLANES, :], jnp.uint32)
    always = lax.shift_right_logical(lax.shift_right_logical(probe, jnp.uint32(16)), jnp.uint32(16)) == 0
    head = jnp.where(always, hmid[0:2 * SUBLANES, 0:LANES], jnp.zeros((2 * SUBLANES, LANES), BF16))
    hmid = jnp.concatenate(
        [jnp.concatenate([head, hmid[0:2 * SUBLANES, LANES:]], axis=1), hmid[2 * SUBLANES:]], axis=0)
    for c in range(d // EXPERT_CHUNK):
        cs = slice(c * EXPERT_CHUNK, (c + 1) * EXPERT_CHUNK)
        y = jnp.dot(hmid, wbuf[2, :, cs], preferred_element_type=F32) + bd_ref[0, :, cs]
        _store_row_tiles(ybuf, (xs_cur,), y, first=c * (EXPERT_CHUNK // LANES))

    @pl.when(s == last)
    def _():
        for sl in ((s + 1) % 3, xs_new):
            gather_block(sl).wait()
            scatter_block(sl).wait()


def _experts(hn, blk_e, tok_idx, dst_idx, w_gate, b_gate, w_up, b_up, w_down, b_down, n_rows_out):
    n_e, d, dff = w_gate.shape
    assert d == dff
    n_steps = blk_e.shape[0]
    blk = EXPERT_BLOCK
    wspec = lambda a, b: pl.BlockSpec((1, a, b), lambda i, be: (be[i], 0, 0))
    ispec = lambda o: pl.BlockSpec((1, 1, blk), lambda i, be, o=o: (i + o, 0, 0), memory_space=pltpu.SMEM)
    return pl.pallas_call(
        _expert_kernel,
        out_shape=jax.ShapeDtypeStruct((n_rows_out * SUBLANES, LANES), F32),
        grid_spec=pltpu.PrefetchScalarGridSpec(
            num_scalar_prefetch=1,
            grid=(n_steps,),
            in_specs=[ispec(0), ispec(1), ispec(2), ispec(0), pl.BlockSpec(memory_space=pl.ANY),
                      wspec(d, dff), wspec(1, dff), wspec(d, dff), wspec(1, dff), wspec(dff, d), wspec(1, d)],
            out_specs=pl.BlockSpec(memory_space=pl.ANY),
            scratch_shapes=[pltpu.VMEM((3, blk * SUBLANES, LANES), F32), pltpu.VMEM((3, blk * SUBLANES, LANES), F32),
                            pltpu.VMEM((blk, dff), BF16), pltpu.VMEM((3, d, dff), BF16),
                            pltpu.SemaphoreType.DMA((3,)), pltpu.SemaphoreType.DMA((3,))],
        ),
        compiler_params=_cparams("arbitrary"),
        name="experts",
    )(blk_e, tok_idx, tok_idx, tok_idx, dst_idx, hn,
      w_gate, b_gate.reshape(n_e, 1, dff), w_up, b_up.reshape(n_e, 1, dff), w_down, b_down.reshape(n_e, 1, d))


def _route(top_idx, n):
    blk = EXPERT_BLOCK
    n4 = n * TOP_K
    n_blocks = n4 // blk + N_EXPERTS
    flat_e = top_idx.reshape(-1)
    skeys = jnp.sort(flat_e * n4 + jnp.arange(n4, dtype=jnp.int32))
    s_flat = skeys % n4
    counts = jnp.sum((flat_e[:, None] == jnp.arange(N_EXPERTS, dtype=jnp.int32)[None, :]).astype(jnp.int32), axis=0)
    starts = jnp.cumsum(counts) - counts
    pcounts = (counts + blk - 1) // blk * blk
    pends = jnp.cumsum(pcounts)
    pstarts = pends - pcounts
    b0 = jnp.arange(n_blocks + 3, dtype=jnp.int32) * blk
    blk_e = jnp.minimum(jnp.sum((b0[:, None] >= pends[None, :]).astype(jnp.int32), axis=1), N_EXPERTS - 1)
    boff = b0 - pstarts[blk_e]
    blk_nv = jnp.clip(counts[blk_e] - boff, 0, blk).astype(jnp.int32)
    j = jnp.arange(blk, dtype=jnp.int32)[None, :]
    valid = j < blk_nv[:, None]
    src = jnp.clip((starts[blk_e] + boff)[:, None] + j, 0, n4 - 1)
    flat = s_flat[src]
    tok = (jnp.where(valid, flat // TOP_K, 0) * SUBLANES).astype(jnp.int32)
    dump = n4 + (jnp.arange(n_blocks + 3, dtype=jnp.int32)[:, None] % 2) * blk + j
    dst = (jnp.where(valid, (flat % TOP_K) * n + flat // TOP_K, dump) * SUBLANES).astype(jnp.int32)
    dst_prev = jnp.concatenate([(n4 + blk + j) * SUBLANES, dst[:n_blocks]], axis=0)
    return blk_e[:n_blocks + 1], tok.reshape(n_blocks + 3, 1, blk), dst_prev.reshape(n_blocks + 1, 1, blk)


def _combine_kernel(h_ref, gate_ref, y0_ref, y1_ref, y2_ref, y3_ref, g_ref, op_ref, os_ref, *, n_first):
    gates = gate_ref[...]
    y = h_ref[...]
    for k, y_ref in enumerate((y0_ref, y1_ref, y2_ref, y3_ref)):
        y = y + gates[:, k:k + 1] * _load_row_tiles(y_ref, (), y.shape[0], F32)
    out = y * lax.rsqrt(jnp.mean(y * y, axis=-1, keepdims=True) + RMS_EPS) * g_ref[...]
    i = pl.program_id(0)

    @pl.when(i < n_first)
    def _():
        op_ref[...] = out

    @pl.when(i >= n_first)
    def _():
        os_ref[...] = out


def _combine(h, gates, ys, final_g, n_prompt):
    n, d = h.shape
    tm = TOKEN_TILE
    nt = n // tm
    n_first = n_prompt // tm
    yspec = lambda k: pl.BlockSpec((tm * SUBLANES, LANES), lambda i, k=k: (k * nt + i, 0))
    return pl.pallas_call(
        functools.partial(_combine_kernel, n_first=n_first),
        out_shape=(jax.ShapeDtypeStruct((n_prompt, d), F32), jax.ShapeDtypeStruct((n - n_prompt, d), F32)),
        grid=(nt,),
        in_specs=[pl.BlockSpec((tm, d), lambda i: (i, 0)), pl.BlockSpec((tm, LANES), lambda i: (i, 0)),
                  yspec(0), yspec(1), yspec(2), yspec(3), pl.BlockSpec((1, d), lambda i: (0, 0))],
        out_specs=_two_batch_specs(n_first, d),
        compiler_params=_cparams("arbitrary"),
        name="combine",
    )(h, gates, ys, ys, ys, ys, final_g.reshape(1, d))


def kernel(x_prompt, x_sample, norm1_g, w_in, rpb, gnorm_na, gnorm_dil, w_out, norm2_g, w_router, b_router,
           w_gate, b_gate, w_up, b_up, w_down, b_down, final_g):
    (bp, t, d), bs = x_prompt.shape, x_sample.shape[0]
    b = bp + bs
    n = b * t
    xp, xs = x_prompt.reshape(bp * t, d), x_sample.reshape(bs * t, d)
    qkv3 = _qkv_proj(xp, xs, norm1_g[0], w_in[0]).reshape(b, t, 3 * MIX_WIDTH)
    ona = _na_attention(qkv3, _na_bias_table(rpb[0])).reshape(n, MIX_NA)
    odil = _dilated_attention(qkv3).reshape(n, MIX_DIL)
    h, hn, idx_pad, gate_pad = _mix_outproj_router(ona, odil, xp, xs, gnorm_na[0], gnorm_dil[0], w_out[0],
                                                   norm2_g[0], w_router[0], b_router[0])
    blk_e, tok_idx, dst_idx = _route(idx_pad[:, :TOP_K], n)
    ys = _experts(hn, blk_e, tok_idx, dst_idx, w_gate[0], b_gate[0], w_up[0], b_up[0], w_down[0], b_down[0],
                  n * TOP_K + 2 * EXPERT_BLOCK)
    yp, ysm = _combine(h, gate_pad, ys, final_g, bp * t)
    return (yp.reshape(bp, t, d), ysm.reshape(bs, t, d))
```

```python
import functools

import jax
import jax.numpy as jnp
from jax import lax
from jax.experimental import pallas as pl
from jax.experimental.pallas import tpu as pltpu

F32 = jnp.float32
BF16 = jnp.bfloat16

HEAD_DIM = 64
N_HEADS_NA = 8
N_HEADS_DIL = 8
MIX_NA = N_HEADS_NA * HEAD_DIM
MIX_DIL = N_HEADS_DIL * HEAD_DIM
MIX_WIDTH = MIX_NA + MIX_DIL
GRID_W = 64
NA_ROWS = 8
NA_COLS = 16
DIL_PATTERNS = ((128, 1), (512, 4), (2048, 16))
N_EXPERTS = 32
TOP_K = 4
SWIGLU_LIMIT = 7.0
SWIGLU_ALPHA = 1.702
RMS_EPS = 1e-5

LANES = 128
NEG_BIG = -1e30
LOG2E = 1.4426950408889634
VMEM_LIMIT = 56 * 1024 * 1024
TOKEN_TILE = 512
EXPERT_BLOCK = 256
EXPERT_CHUNK = 256
SUBLANES = 8
ATTN_BLOCKS = 4
DIL_QB = 128
DIL_MIX_ROWS = 256
DIST_MASKED = 1e30


def _cparams(*sem):
    return pltpu.CompilerParams(dimension_semantics=sem, vmem_limit_bytes=VMEM_LIMIT)


def _store_row_tiles(ref, lead, val, first=0):
    rows = val.shape[0]
    for s in range(val.shape[1] // LANES):
        ref[lead + (pl.ds(first + s, rows, stride=SUBLANES), slice(None))] = val[:, s * LANES:(s + 1) * LANES]


def _load_row_tiles(ref, lead, rows, dtype):
    return jnp.concatenate(
        [ref[lead + (pl.ds(s, rows, stride=SUBLANES), slice(None))].astype(dtype) for s in range(SUBLANES)],
        axis=-1)


def _two_batch_specs(n_first, width):
    first = pl.BlockSpec((TOKEN_TILE, width), lambda i: (jnp.minimum(i, n_first - 1), 0))
    second = pl.BlockSpec((TOKEN_TILE, width), lambda i: (jnp.maximum(i - n_first, 0), 0))
    return first, second


def _qkv_kernel(xp_ref, xs_ref, g_ref, w_ref, o_ref, *, n_first):
    x = jnp.where(pl.program_id(0) < n_first, xp_ref[...], xs_ref[...])
    ms = jnp.mean(x * x, axis=-1, keepdims=True)
    xn = (x * lax.rsqrt(ms + RMS_EPS) * g_ref[...]).astype(BF16)
    n_out = o_ref.shape[1]
    chunk = 512
    for j in range(n_out // chunk):
        acc = jnp.dot(xn, w_ref[:, j * chunk:(j + 1) * chunk], preferred_element_type=F32)
        if (j + 1) * chunk <= MIX_WIDTH:
            acc = acc * (HEAD_DIM ** -0.5 * LOG2E)
        o_ref[:, j * chunk:(j + 1) * chunk] = acc.astype(BF16)


def _qkv_proj(xp, xs, g1, w_in):
    d = xp.shape[1]
    n = xp.shape[0] + xs.shape[0]
    n_out = w_in.shape[1]
    n_first = xp.shape[0] // TOKEN_TILE
    return pl.pallas_call(
        functools.partial(_qkv_kernel, n_first=n_first),
        out_shape=jax.ShapeDtypeStruct((n, n_out), BF16),
        grid=(n // TOKEN_TILE,),
        in_specs=[
            *_two_batch_specs(n_first, d),
            pl.BlockSpec((1, d), lambda i: (0, 0)),
            pl.BlockSpec((d, n_out), lambda i: (0, 0)),
        ],
        out_specs=pl.BlockSpec((TOKEN_TILE, n_out), lambda i: (i, 0)),
        compiler_params=_cparams("arbitrary"),
        name="qkv_proj",
    )(xp, xs, g1.reshape(1, d), w_in.astype(BF16))


def _na_bias_table(rpb):
    kw = NA_COLS
    cols = jnp.arange(GRID_W)
    col_start = jnp.clip(cols - kw // 2, 0, GRID_W - kw)
    col_ok = (cols[None, :] >= col_start[:, None]) & (cols[None, :] < col_start[:, None] + kw)
    dc_idx = jnp.clip(cols[None, :] - cols[:, None] + (kw - 1), 0, 2 * kw - 2)
    b = rpb.astype(F32)[:, :, dc_idx]
    b = jnp.where(col_ok[None, None], b * LOG2E, NEG_BIG)
    slabs = [b[:, o:o + NA_ROWS] for o in range(NA_ROWS)]
    t = jnp.stack(slabs, 0)
    t = jnp.transpose(t, (0, 1, 3, 2, 4))
    return t.reshape(NA_ROWS, rpb.shape[0] // 2, 2 * GRID_W, NA_ROWS * GRID_W)


def _attend(qs, ks, vs, add_bias):
    nt = (((1,), (1,)), ((), ()))
    ss = [add_bias(u, lax.dot_general(q, k, nt, preferred_element_type=F32)) for u, (q, k) in enumerate(zip(qs, ks))]
    ms = [jnp.max(s, axis=-1, keepdims=True) for s in ss]
    ps = [jnp.exp2(s - m) for s, m in zip(ss, ms)]
    ls = [jnp.sum(p, axis=-1, keepdims=True) for p in ps]
    os = [jnp.dot(p.astype(BF16), v, preferred_element_type=F32) for p, v in zip(ps, vs)]
    return os, ms, ls


def _stack_heads(q, head0):
    zero = jnp.zeros_like(q)
    return jnp.concatenate([jnp.where(head0, q, zero), jnp.where(head0, zero, q)], axis=0)


def _unstack_heads(a, head0, rows):
    return jnp.where(head0, a[:rows], a[rows:])


def _na_kernel(q_ref, k_ref, v_ref, b_ref, o_ref):
    t = q_ref.shape[1]
    n_rows = t // GRID_W
    kh = NA_ROWS
    head0 = lax.broadcasted_iota(jnp.int32, (GRID_W, LANES), 1) < HEAD_DIM

    def rows_body(it, carry):
        rows = [it * ATTN_BLOCKS + u for u in range(ATTN_BLOCKS)]
        starts = [jnp.clip(r - kh // 2, 0, n_rows - kh) for r in rows]
        offs = [rs - r + (NA_ROWS - 1) for r, rs in zip(rows, starts)]
        q0s = [pl.multiple_of(r * GRID_W, GRID_W) for r in rows]
        k0s = [pl.multiple_of(rs * GRID_W, GRID_W) for rs in starts]
        qs = [_stack_heads(q_ref[0, pl.ds(q0, GRID_W), :], head0) for q0 in q0s]
        ks = [k_ref[0, pl.ds(k0, kh * GRID_W), :] for k0 in k0s]
        vs = [v_ref[0, pl.ds(k0, kh * GRID_W), :] for k0 in k0s]
        outs, _, sums = _attend(qs, ks, vs, lambda u, s: s + b_ref[offs[u], 0])
        for q0, o2, l2 in zip(q0s, outs, sums):
            o = _unstack_heads(o2, head0, GRID_W) / _unstack_heads(l2, head0, GRID_W)
            o_ref[0, pl.ds(q0, GRID_W), :] = o.astype(o_ref.dtype)
        return carry

    lax.fori_loop(0, n_rows // ATTN_BLOCKS, rows_body, 0)


def _na_attention(qkv3, bias_tab):
    b, t, _ = qkv3.shape
    n_pair = MIX_NA // LANES
    col = MIX_WIDTH // LANES
    spec = lambda c: pl.BlockSpec((1, t, LANES), lambda p, i, c=c: (i, 0, c + p))
    return pl.pallas_call(
        _na_kernel,
        out_shape=jax.ShapeDtypeStruct((b, t, MIX_NA), BF16),
        grid=(n_pair, b),
        in_specs=[spec(0), spec(col), spec(2 * col),
                  pl.BlockSpec((NA_ROWS, 1) + bias_tab.shape[2:], lambda p, i: (0, p, 0, 0))],
        out_specs=pl.BlockSpec((1, t, LANES), lambda p, i: (i, 0, p)),
        compiler_params=_cparams("arbitrary", "arbitrary"),
        name="na_attention",
    )(qkv3, qkv3, qkv3, bias_tab)


def _dil_kernel(q_ref, k_ref, v_ref, o_ref, stage, perm, ob, lb, bias):
    t = q_ref.shape[1]
    p = pl.program_id(0)
    qb = DIL_QB
    half = DIL_PATTERNS[0][0] // (2 * DIL_PATTERNS[0][1])
    kmax = qb + 2 * half
    head0 = lax.broadcasted_iota(jnp.int32, (qb, LANES), 1) < HEAD_DIM

    row = lax.broadcasted_iota(jnp.int32, (2 * qb, kmax), 0)
    col = lax.broadcasted_iota(jnp.int32, (2 * qb, kmax), 1)
    qrow = jnp.where(row < qb, row, row - qb)
    hrow = lax.broadcasted_iota(jnp.int32, (2 * qb, 1), 0)
    head = 2 * jnp.full((2 * qb, 1), p, jnp.int32) + jnp.where(hrow < qb, 0, 1)
    slope = jnp.exp2((head + 1).astype(F32) * (-8.0 / N_HEADS_DIL)) * LOG2E
    dists = []
    for rel in (0, -half, qb - kmax):
        d = jnp.abs(col - qrow + rel)
        dists.append(jnp.where(d <= half, d.astype(F32), DIST_MASKED))

    srcs = (q_ref, k_ref, v_ref)
    for a in range(3):
        stage[0, a] = srcs[a][0].astype(F32)
    n_perm = 0
    prev_dil = 1
    for bi, (window, dil) in enumerate(DIL_PATTERNS):
        assert window // (2 * dil) == half
        s_len = t // dil
        if dil == 1:
            views = tuple(r.at[0] for r in srcs)
        else:
            f = dil // prev_dil
            assert dil == f * prev_dil
            for a in range(3):
                for r_prev in range(prev_dil):
                    for q in range(f):
                        rows = stage[n_perm, a, pl.ds(r_prev * (t // prev_dil) + q, s_len, stride=f), :]
                        dst = pl.ds((r_prev + prev_dil * q) * s_len, s_len)
                        if bi + 1 < len(DIL_PATTERNS):
                            stage[n_perm + 1, a, dst, :] = rows
                        perm[n_perm, a, dst, :] = rows.astype(BF16)
            views = tuple(perm.at[n_perm, a] for a in range(3))
            n_perm += 1
            prev_dil = dil
        qsrc, ksrc, vsrc = views
        n_qb = s_len // qb
        kwin = min(kmax, s_len)
        assert n_qb == 1 or n_qb >= 3
        for v, dm in enumerate(dists):
            bias[v] = (slope * float(dil)) * dm

        def units(it, carry, bi=bi, dil=dil, s_len=s_len, n_qb=n_qb, kwin=kwin, qsrc=qsrc, ksrc=ksrc, vsrc=vsrc):
            qrs, krs, vidx, outrows = [], [], [], []
            for u in range(ATTN_BLOCKS):
                blk = it * ATTN_BLOCKS + u
                if n_qb == 1:
                    g, qs, ks, v = blk, 0, 0, 0
                else:
                    g = blk // n_qb
                    j = blk % n_qb
                    qs = j * qb
                    ks = jnp.clip(qs - half, 0, s_len - kwin)
                    v = jnp.where(j == 0, 0, jnp.where(j == n_qb - 1, 2, 1))
                qr = pl.multiple_of(g * s_len + qs, qb)
                qrs.append(qr)
                krs.append(pl.multiple_of(g * s_len + ks, half))
                vidx.append(v)
                outrows.append(pl.ds(qr, qb) if dil == 1 else pl.ds(qs * dil + g, qb, stride=dil))
            qs_ = [_stack_heads(qsrc[pl.ds(qr, qb), :], head0) for qr in qrs]
            ks_ = [ksrc[pl.ds(kr, kwin), :] for kr in krs]
            vs_ = [vsrc[pl.ds(kr, kwin), :] for kr in krs]
            outs, maxs, sums = _attend(qs_, ks_, vs_, lambda u, s: s - bias[vidx[u], :, :kwin])
            for rows, o2, m2, l2 in zip(outrows, outs, maxs, sums):
                l = _unstack_heads(l2, head0, qb)
                ob[bi, rows, :] = _unstack_heads(o2, head0, qb) / l
                lb[bi, rows, :] = _unstack_heads(m2, head0, qb) + jnp.log2(l)
            return carry

        assert (dil * n_qb) % ATTN_BLOCKS == 0
        lax.fori_loop(0, dil * n_qb // ATTN_BLOCKS, units, 0)

    def mix(c, carry):
        rows = pl.ds(pl.multiple_of(c * DIL_MIX_ROWS, DIL_MIX_ROWS), DIL_MIX_ROWS)
        l1, l2, l3 = lb[0, rows, :], lb[1, rows, :], lb[2, rows, :]
        m = jnp.maximum(jnp.maximum(l1, l2), l3)
        e1, e2, e3 = jnp.exp2(l1 - m), jnp.exp2(l2 - m), jnp.exp2(l3 - m)
        o = (e1 * ob[0, rows, :] + e2 * ob[1, rows, :] + e3 * ob[2, rows, :]) / (e1 + e2 + e3)
        o_ref[0, rows, :] = o.astype(o_ref.dtype)
        return carry

    lax.fori_loop(0, t // DIL_MIX_ROWS, mix, 0)


def _dilated_attention(qkv3):
    b, t, _ = qkv3.shape
    n_pair = MIX_DIL // LANES
    col = MIX_WIDTH // LANES
    first = MIX_NA // LANES
    spec = lambda c: pl.BlockSpec((1, t, LANES), lambda p, i, c=c: (i, 0, first + c + p))
    n_perm = sum(1 for _, dil in DIL_PATTERNS if dil > 1)
    n_br = len(DIL_PATTERNS)
    kmax = DIL_QB + DIL_PATTERNS[0][0] // DIL_PATTERNS[0][1]
    return pl.pallas_call(
        _dil_kernel,
        out_shape=jax.ShapeDtypeStruct((b, t, MIX_DIL), BF16),
        grid=(n_pair, b),
        in_specs=[spec(0), spec(col), spec(2 * col)],
        out_specs=pl.BlockSpec((1, t, LANES), lambda p, i: (i, 0, p)),
        scratch_shapes=[pltpu.VMEM((n_perm, 3, t, LANES), F32), pltpu.VMEM((n_perm, 3, t, LANES), BF16),
                        pltpu.VMEM((n_br, t, LANES), F32), pltpu.VMEM((n_br, t, LANES), F32),
                        pltpu.VMEM((3, 2 * DIL_QB, kmax), F32)],
        compiler_params=_cparams("arbitrary", "arbitrary"),
        name="dilated_attention",
    )(qkv3, qkv3, qkv3)


def _split_bf16(a):
    hi = a.astype(BF16)
    lo = (a - hi.astype(F32)).astype(BF16)
    return hi, lo


def _mix_kernel(ona_ref, odil_ref, xp_ref, xs_ref, gna_ref, gdil_ref, wout_ref, g2_ref, wr_hi_ref, wr_lo_ref,
                br_ref, h_ref, hn_ref, idx_ref, gate_ref, *, n_first):
    ona = ona_ref[...].astype(F32)
    odil = odil_ref[...].astype(F32)
    x = jnp.where(pl.program_id(0) < n_first, xp_ref[...], xs_ref[...])

    def gnorm(a, g):
        return a * lax.rsqrt(jnp.mean(a * a, axis=-1, keepdims=True) + RMS_EPS) * g

    mixed = jnp.concatenate([gnorm(ona, gna_ref[...]), gnorm(odil, gdil_ref[...])], axis=-1)
    h = x + jnp.dot(mixed.astype(BF16), wout_ref[...], preferred_element_type=F32)
    h_ref[...] = h
    hn = gnorm(h, g2_ref[...])
    _store_row_tiles(hn_ref, (), hn)

    hi, lo = _split_bf16(hn)
    logits = (jnp.dot(hi, wr_hi_ref[...], preferred_element_type=F32)
              + jnp.dot(hi, wr_lo_ref[...], preferred_element_type=F32)
              + jnp.dot(lo, wr_hi_ref[...], preferred_element_type=F32)) + br_ref[...]

    n_e = logits.shape[-1]
    eidx = lax.broadcasted_iota(jnp.int32, logits.shape, 1).astype(F32)
    vals, idxs = [], []
    cur = logits
    for _ in range(TOP_K):
        mx = jnp.max(cur, axis=-1, keepdims=True)
        ix = jnp.min(jnp.where(cur == mx, eidx, float(n_e)), axis=-1, keepdims=True)
        vals.append(mx)
        idxs.append(ix)
        cur = jnp.where(eidx == ix, -jnp.inf, cur)
    es = [jnp.exp(v - vals[0]) for v in vals]
    tot = es[0] + es[1] + es[2] + es[3]
    lane = lax.broadcasted_iota(jnp.int32, idx_ref.shape, 1)
    idx_out = jnp.zeros(idx_ref.shape, F32)
    gate_out = jnp.zeros(gate_ref.shape, F32)
    for k in range(TOP_K):
        idx_out = jnp.where(lane == k, idxs[k], idx_out)
        gate_out = jnp.where(lane == k, es[k] / tot, gate_out)
    idx_ref[...] = idx_out.astype(jnp.int32)
    gate_ref[...] = gate_out


def _mix_outproj_router(ona, odil, xp, xs, gna, gdil, w_out, g2, w_router, b_router):
    d = xp.shape[1]
    n = xp.shape[0] + xs.shape[0]
    tm = TOKEN_TILE
    n_first = xp.shape[0] // tm
    wr_hi, wr_lo = _split_bf16(w_router)
    row = lambda w: pl.BlockSpec((tm, w), lambda i: (i, 0))
    full = lambda a: pl.BlockSpec(a.shape, lambda i: (0,) * a.ndim)
    gna2, gdil2, g22, br2 = gna.reshape(1, -1), gdil.reshape(1, -1), g2.reshape(1, -1), b_router.reshape(1, -1)
    wout_b = w_out.astype(BF16)
    return pl.pallas_call(
        functools.partial(_mix_kernel, n_first=n_first),
        out_shape=(jax.ShapeDtypeStruct((n, d), F32), jax.ShapeDtypeStruct((n * SUBLANES, LANES), F32),
                   jax.ShapeDtypeStruct((n, LANES), jnp.int32), jax.ShapeDtypeStruct((n, LANES), F32)),
        grid=(n // tm,),
        in_specs=[row(MIX_NA), row(MIX_DIL), *_two_batch_specs(n_first, d),
                  full(gna2), full(gdil2), full(wout_b), full(g22), full(wr_hi), full(wr_lo), full(br2)],
        out_specs=(row(d), pl.BlockSpec((tm * SUBLANES, LANES), lambda i: (i, 0)), row(LANES), row(LANES)),
        compiler_params=_cparams("arbitrary"),
        name="mix_outproj_router",
    )(ona, odil, xp, xs, gna2, gdil2, wout_b, g22, wr_hi, wr_lo, br2)


def _expert_kernel(be_ref, tok0_ref, tok1_ref, tok2_ref, dstp_ref, hn_hbm, wg_ref, bg_ref, wu_ref, bu_ref, wd_ref,
                   bd_ref, ys_hbm, xbuf, ybuf, hbuf, wbuf, gsem, ssem):
    s = pl.program_id(0)
    last = pl.num_programs(0) - 1
    xs_cur = s % 3
    xs_new = (s + 2) % 3
    ys_old = (s + 2) % 3
    blk, dff = hbuf.shape
    d = wbuf.shape[2]
    tile = SUBLANES

    def gather_row(tok_ref, j, sl):
        src = pl.multiple_of(tok_ref[0, 0, j], tile)
        return pltpu.make_async_copy(hn_hbm.at[pl.ds(src, tile), :], xbuf.at[sl, pl.ds(j * tile, tile), :],
                                     gsem.at[sl])

    def scatter_row(j, sl, shift=0):
        dst = pl.multiple_of(dstp_ref[0, 0, j] + shift, tile)
        return pltpu.make_async_copy(ybuf.at[sl, pl.ds(j * tile, tile), :], ys_hbm.at[pl.ds(dst, tile), :],
                                     ssem.at[sl])

    def gather_block(sl):
        return pltpu.make_async_copy(hn_hbm.at[pl.ds(0, blk * tile), :], xbuf.at[sl], gsem.at[sl])

    def scatter_block(sl):
        return pltpu.make_async_copy(ybuf.at[sl], ys_hbm.at[pl.ds(0, blk * tile), :], ssem.at[sl])

    @pl.when(s == 0)
    def _():
        ybuf[...] = jnp.zeros_like(ybuf)

        def first(j, c):
            gather_row(tok0_ref, j, 0).start()
            gather_row(tok1_ref, j, 1).start()
            scatter_row(j, 0).start()
            scatter_row(j, 1, shift=-blk * tile).start()
            return c

        lax.fori_loop(0, blk, first, 0)

    gather_block(xs_cur).wait()
    scatter_block(xs_cur).wait()

    @pl.when((s == 0) | (be_ref[s] != be_ref[jnp.maximum(s - 1, 0)]))
    def _():
        for i, w_ref in enumerate((wg_ref, wu_ref, wd_ref)):
            wbuf[i] = w_ref[0].astype(BF16)

    for j in range(blk):
        scatter_row(j, ys_old).start(priority=j % 2)

    x = _load_row_tiles(xbuf, (xs_cur,), blk, BF16)
    for c in range(dff // EXPERT_CHUNK):
        cs = slice(c * EXPERT_CHUNK, (c + 1) * EXPERT_CHUNK)
        g = jnp.dot(x, wbuf[0, :, cs], preferred_element_type=F32) + bg_ref[0, :, cs]
        u = jnp.dot(x, wbuf[1, :, cs], preferred_element_type=F32) + bu_ref[0, :, cs]
        g = jnp.minimum(g, SWIGLU_LIMIT)
        u = jnp.clip(u, -SWIGLU_LIMIT, SWIGLU_LIMIT)
        hbuf[:, cs] = ((u + 1.0) * (g * jax.nn.sigmoid(SWIGLU_ALPHA * g))).astype(BF16)
    for j in range(blk):
        gather_row(tok2_ref, j, xs_new).start(priority=j % 2)
    hmid = hbuf[...]
    probe = pltpu.bitcast(xbuf[xs_cur, 0:2 * SUBLANES, :], jnp.uint32)
    always = lax.shift_right_logical(lax.shift_right_logical(probe, jnp.uint32(16)), jnp.uint32(16)) == 0
    head = jnp.where(always, hmid[0:2 * SUBLANES, 0:LANES], jnp.zeros((2 * SUBLANES, LANES), BF16))
    hmid = jnp.concatenate(
        [jnp.concatenate([head, hmid[0:2 * SUBLANES, LANES:]], axis=1), hmid[2 * SUBLANES:]], axis=0)
    for c in range(d // EXPERT_CHUNK):
        cs = slice(c * EXPERT_CHUNK, (c + 1) * EXPERT_CHUNK)
        y = jnp.dot(hmid, wbuf[2, :, cs], preferred_element_type=F32) + bd_ref[0, :, cs]
        _store_row_tiles(ybuf, (xs_cur,), y, first=c * (EXPERT_CHUNK // LANES))

    @pl.when(s == last)
    def _():
        for sl in ((s + 1) % 3, xs_new):
            gather_block(sl).wait()
            scatter_block(sl).wait()


def _experts(hn, blk_e, tok_idx, dst_idx, w_gate, b_gate, w_up, b_up, w_down, b_down, n_rows_out):
    n_e, d, dff = w_gate.shape
    assert d == dff
    n_steps = blk_e.shape[0]
    blk = EXPERT_BLOCK
    wspec = lambda a, b: pl.BlockSpec((1, a, b), lambda i, be: (be[i], 0, 0))
    ispec = lambda o: pl.BlockSpec((1, 1, blk), lambda i, be, o=o: (i + o, 0, 0), memory_space=pltpu.SMEM)
    return pl.pallas_call(
        _expert_kernel,
        out_shape=jax.ShapeDtypeStruct((n_rows_out * SUBLANES, LANES), F32),
        grid_spec=pltpu.PrefetchScalarGridSpec(
            num_scalar_prefetch=1,
            grid=(n_steps,),
            in_specs=[ispec(0), ispec(1), ispec(2), ispec(0), pl.BlockSpec(memory_space=pl.ANY),
                      wspec(d, dff), wspec(1, dff), wspec(d, dff), wspec(1, dff), wspec(dff, d), wspec(1, d)],
            out_specs=pl.BlockSpec(memory_space=pl.ANY),
            scratch_shapes=[pltpu.VMEM((3, blk * SUBLANES, LANES), F32), pltpu.VMEM((3, blk * SUBLANES, LANES), F32),
                            pltpu.VMEM((blk, dff), BF16), pltpu.VMEM((3, d, dff), BF16),
                            pltpu.SemaphoreType.DMA((3,)), pltpu.SemaphoreType.DMA((3,))],
        ),
        compiler_params=_cparams("arbitrary"),
        name="experts",
    )(blk_e, tok_idx, tok_idx, tok_idx, dst_idx, hn,
      w_gate, b_gate.reshape(n_e, 1, dff), w_up, b_up.reshape(n_e, 1, dff), w_down, b_down.reshape(n_e, 1, d))


def _route(top_idx, n):
    blk = EXPERT_BLOCK
    n4 = n * TOP_K
    n_blocks = n4 // blk + N_EXPERTS
    flat_e = top_idx.reshape(-1)
    skeys = jnp.sort(flat_e * n4 + jnp.arange(n4, dtype=jnp.int32))
    s_flat = skeys % n4
    counts = jnp.sum((flat_e[:, None] == jnp.arange(N_EXPERTS, dtype=jnp.int32)[None, :]).astype(jnp.int32), axis=0)
    starts = jnp.cumsum(counts) - counts
    pcounts = (counts + blk - 1) // blk * blk
    pends = jnp.cumsum(pcounts)
    pstarts = pends - pcounts
    b0 = jnp.arange(n_blocks + 3, dtype=jnp.int32) * blk
    blk_e = jnp.minimum(jnp.sum((b0[:, None] >= pends[None, :]).astype(jnp.int32), axis=1), N_EXPERTS - 1)
    boff = b0 - pstarts[blk_e]
    blk_nv = jnp.clip(counts[blk_e] - boff, 0, blk).astype(jnp.int32)
    j = jnp.arange(blk, dtype=jnp.int32)[None, :]
    valid = j < blk_nv[:, None]
    src = jnp.clip((starts[blk_e] + boff)[:, None] + j, 0, n4 - 1)
    flat = s_flat[src]
    tok = (jnp.where(valid, flat // TOP_K, 0) * SUBLANES).astype(jnp.int32)
    dump = n4 + (jnp.arange(n_blocks + 3, dtype=jnp.int32)[:, None] % 2) * blk + j
    dst = (jnp.where(valid, (flat % TOP_K) * n + flat // TOP_K, dump) * SUBLANES).astype(jnp.int32)
    dst_prev = jnp.concatenate([(n4 + blk + j) * SUBLANES, dst[:n_blocks]], axis=0)
    return blk_e[:n_blocks + 1], tok.reshape(n_blocks + 3, 1, blk), dst_prev.reshape(n_blocks + 1, 1, blk)


def _combine_kernel(h_ref, gate_ref, y0_ref, y1_ref, y2_ref, y3_ref, g_ref, op_ref, os_ref, *, n_first):
    gates = gate_ref[...]
    y = h_ref[...]
    for k, y_ref in enumerate((y0_ref, y1_ref, y2_ref, y3_ref)):
        y = y + gates[:, k:k + 1] * _load_row_tiles(y_ref, (), y.shape[0], F32)
    out = y * lax.rsqrt(jnp.mean(y * y, axis=-1, keepdims=True) + RMS_EPS) * g_ref[...]
    i = pl.program_id(0)

    @pl.when(i < n_first)
    def _():
        op_ref[...] = out

    @pl.when(i >= n_first)
    def _():
        os_ref[...] = out


def _combine(h, gates, ys, final_g, n_prompt):
    n, d = h.shape
    tm = TOKEN_TILE
    nt = n // tm
    n_first = n_prompt // tm
    yspec = lambda k: pl.BlockSpec((tm * SUBLANES, LANES), lambda i, k=k: (k * nt + i, 0))
    return pl.pallas_call(
        functools.partial(_combine_kernel, n_first=n_first),
        out_shape=(jax.ShapeDtypeStruct((n_prompt, d), F32), jax.ShapeDtypeStruct((n - n_prompt, d), F32)),
        grid=(nt,),
        in_specs=[pl.BlockSpec((tm, d), lambda i: (i, 0)), pl.BlockSpec((tm, LANES), lambda i: (i, 0)),
                  yspec(0), yspec(1), yspec(2), yspec(3), pl.BlockSpec((1, d), lambda i: (0, 0))],
        out_specs=_two_batch_specs(n_first, d),
        compiler_params=_cparams("arbitrary"),
        name="combine",
    )(h, gates, ys, ys, ys, ys, final_g.reshape(1, d))


def kernel(x_prompt, x_sample, norm1_g, w_in, rpb, gnorm_na, gnorm_dil, w_out, norm2_g, w_router, b_router,
           w_gate, b_gate, w_up, b_up, w_down, b_down, final_g):
    (bp, t, d), bs = x_prompt.shape, x_sample.shape[0]
    b = bp + bs
    n = b * t
    xp, xs = x_prompt.reshape(bp * t, d), x_sample.reshape(bs * t, d)
    qkv3 = _qkv_proj(xp, xs, norm1_g[0], w_in[0]).reshape(b, t, 3 * MIX_WIDTH)
    ona = _na_attention(qkv3, _na_bias_table(rpb[0])).reshape(n, MIX_NA)
    odil = _dilated_attention(qkv3).reshape(n, MIX_DIL)
    h, hn, idx_pad, gate_pad = _mix_outproj_router(ona, odil, xp, xs, gnorm_na[0], gnorm_dil[0], w_out[0],
                                                   norm2_g[0], w_router[0], b_router[0])
    blk_e, tok_idx, dst_idx = _route(idx_pad[:, :TOP_K], n)
    ys = _experts(hn, blk_e, tok_idx, dst_idx, w_gate[0], b_gate[0], w_up[0], b_up[0], w_down[0], b_down[0],
                  n * TOP_K + 2 * EXPERT_BLOCK)
    yp, ysm = _combine(h, gate_pad, ys, final_g, bp * t)
    return (yp.reshape(bp, t, d), ysm.reshape(bs, t, d))
```
